```python
import jax, jax.numpy as jnp
from jax import lax
import numpy as np

D_MODEL = 1024
BATCH = 8
SEQ = 2048
DEPTH = 4

CHUNK = 64
Q_BLOCK = 128
N_EVEN = (DEPTH + 1) // 2
N_ODD = DEPTH // 2
EPS = 1e-6

CONV_DIM = D_MODEL // 2
CONV_WIDTH = 3
MLA_HEADS = 8
MLA_Q_RANK = D_MODEL // 4
MLA_KV_RANK = D_MODEL // 4
MLA_NOPE = 64
MLA_ROPE = 32
MLA_V = (D_MODEL // 2) // MLA_HEADS
ROPE_THETA = 10000.0
ML_HEADS = 4
ML_DH = D_MODEL // ML_HEADS
D_FF = 4 * D_MODEL
POS_OFFSET_MAX = 4096

HYB_SPLITS = [CONV_DIM, CONV_DIM, CONV_DIM, MLA_Q_RANK, MLA_KV_RANK, MLA_ROPE]
HYB_IN = sum(HYB_SPLITS)
ML_SPLITS = [D_MODEL, D_MODEL, D_MODEL, D_MODEL, ML_HEADS, ML_HEADS]
ML_IN = sum(ML_SPLITS)

kernel_name = 'chunk_causal_conv_mla_mlstm_hybrid'


def _split(a, sizes):
    return jnp.split(a, np.cumsum(sizes)[:-1].tolist(), axis=-1)


def rmsnorm(x, g):
    xf = x.astype(jnp.float32)
    y = xf * lax.rsqrt(jnp.mean(xf * xf, axis=-1, keepdims=True) + EPS)
    return (y * g.astype(jnp.float32)).astype(x.dtype)


def rope(x, positions):
    half = x.shape[-1] // 2
    inv = ROPE_THETA ** (-jnp.arange(half, dtype=jnp.float32) / half)
    ang = positions.astype(jnp.float32)[..., None] * inv
    ang = ang.reshape(ang.shape[:2] + (1,) * (x.ndim - 3) + (half,))
    cos, sin = jnp.cos(ang), jnp.sin(ang)
    xf = x.astype(jnp.float32)
    x1, x2 = xf[..., :half], xf[..., half:]
    return jnp.concatenate([x1 * cos - x2 * sin, x1 * sin + x2 * cos], axis=-1).astype(x.dtype)


def short_conv_mixer(b_gate, c_gate, h, conv_w):
    u = c_gate * h
    y = lax.conv_general_dilated(
        u, conv_w[:, None, :].astype(u.dtype), window_strides=(1,),
        padding=[(CONV_WIDTH - 1, 0)], dimension_numbers=('NWC', 'WIO', 'NWC'),
        feature_group_count=CONV_DIM)
    return b_gate * y


def mla_mixer(c_q, c_kv, k_r, positions, q_norm, kv_norm, w_uq, w_ukv):
    B, S, _ = c_q.shape
    q = jnp.einsum('bsr,rf->bsf', rmsnorm(c_q, q_norm), w_uq).reshape(B, S, MLA_HEADS, MLA_NOPE + MLA_ROPE)
    q_nope, q_rope = q[..., :MLA_NOPE], rope(q[..., MLA_NOPE:], positions)
    kv = jnp.einsum('bsr,rf->bsf', rmsnorm(c_kv, kv_norm), w_ukv).reshape(B, S, MLA_HEADS, MLA_NOPE + MLA_V)
    k_nope, v = kv[..., :MLA_NOPE], kv[..., MLA_NOPE:]
    k_rope = rope(k_r, positions)
    scale = (MLA_NOPE + MLA_ROPE) ** -0.5
    outs = []
    for qb in range(S // Q_BLOCK):
        q0, kv_end = qb * Q_BLOCK, (qb + 1) * Q_BLOCK
        s = (jnp.einsum('bqhd,bkhd->bhqk', q_nope[:, q0:kv_end], k_nope[:, :kv_end])
             + jnp.einsum('bqhd,bkd->bhqk', q_rope[:, q0:kv_end], k_rope[:, :kv_end]))
        s = s.astype(jnp.float32) * scale
        mask = (np.arange(kv_end) // CHUNK)[None, :] <= (np.arange(q0, kv_end) // CHUNK)[:, None]
        p = jax.nn.softmax(jnp.where(mask, s, -jnp.inf), axis=-1).astype(v.dtype)
        outs.append(jnp.einsum('bhqk,bkhd->bqhd', p, v[:, :kv_end]))
    return jnp.concatenate(outs, axis=1).reshape(B, S, MLA_HEADS * MLA_V)


def mlstm_cell(q, k, v, i_pre, f_pre):
    B, S, H, DH = q.shape
    NC, L = S // CHUNK, CHUNK
    to_chunks = lambda a: a.reshape(B, NC, L, H, DH).transpose(0, 3, 1, 2, 4)
    q, k, v = to_chunks(q) * (DH ** -0.5), to_chunks(k), to_chunks(v)
    log_f = jax.nn.log_sigmoid(f_pre).reshape(B, NC, L, H).transpose(0, 3, 1, 2)
    log_i = i_pre.reshape(B, NC, L, H).transpose(0, 3, 1, 2)
    b = jnp.cumsum(log_f, axis=-1)
    causal = np.tril(np.ones((L, L), dtype=bool))
    d_log = jnp.where(causal, b[..., :, None] - b[..., None, :] + log_i[..., None, :], -jnp.inf)
    m_intra = jnp.max(d_log, axis=-1)
    b_last = b[..., -1]
    a = b_last[..., None] - b + log_i
    a_max = jnp.max(a, axis=-1)
    kw = k * jnp.exp(a - a_max[..., None])[..., None]
    c_loc = jnp.einsum('bhcsk,bhcsv->bhckv', kw, v)
    n_loc = jnp.sum(kw, axis=3)

    def step(carry, inp):
        c_st, n_st, m_st = carry
        cl, nl, am, bl = inp
        m_new = jnp.maximum(bl + m_st, am)
        sp, sl = jnp.exp(bl + m_st - m_new), jnp.exp(am - m_new)
        return (sp[..., None, None] * c_st + sl[..., None, None] * cl,
                sp[..., None] * n_st + sl[..., None] * nl, m_new), (c_st, n_st, m_st)

    init = (jnp.zeros((B, H, DH, DH), jnp.float32), jnp.zeros((B, H, DH), jnp.float32),
            jnp.zeros((B, H), jnp.float32))
    xs = (c_loc.transpose(2, 0, 1, 3, 4), n_loc.transpose(2, 0, 1, 3),
          a_max.transpose(2, 0, 1), b_last.transpose(2, 0, 1))
    _, (c_prev, n_prev, m_prev) = lax.scan(step, init, xs)
    c_prev = c_prev.transpose(1, 2, 0, 3, 4)
    n_prev = n_prev.transpose(1, 2, 0, 3)
    m_prev = m_prev.transpose(1, 2, 0)[..., None]

    m_t = jnp.maximum(b + m_prev, m_intra)
    inter = jnp.exp(b + m_prev - m_t)
    scores = jnp.einsum('bhctd,bhcsd->bhcts', q, k) * jnp.exp(d_log - m_t[..., None])
    num = (inter[..., None] * jnp.einsum('bhctk,bhckv->bhctv', q, c_prev)
           + jnp.einsum('bhcts,bhcsv->bhctv', scores, v))
    den = inter * jnp.einsum('bhctk,bhck->bhct', q, n_prev) + jnp.sum(scores, axis=-1)
    h = num / jnp.maximum(jnp.abs(den), jnp.exp(-m_t))[..., None]
    return h.transpose(0, 2, 3, 1, 4).reshape(B, S, H * DH)


def conv_mla_layer(h, positions, w_in, conv_w, q_norm, kv_norm, w_uq, w_ukv, w_out):
    proj = jnp.einsum('bsd,df->bsf', h, w_in)
    b_gate, c_gate, hc, c_q, c_kv, k_r = _split(proj, HYB_SPLITS)
    y_a = short_conv_mixer(b_gate, c_gate, hc, conv_w)
    y_b = mla_mixer(c_q, c_kv, k_r, positions, q_norm, kv_norm, w_uq, w_ukv)
    return jnp.einsum('bsf,fd->bsd', jnp.concatenate([y_a, y_b], axis=-1), w_out)


def mlstm_layer(h, w_in, b_i, b_f, head_norm, w_out):
    B, S, _ = h.shape
    proj = jnp.einsum('bsd,df->bsf', h, w_in)
    q, k, v, o, i_pre, f_pre = _split(proj, ML_SPLITS)
    heads = lambda a: a.astype(jnp.float32).reshape(B, S, ML_HEADS, ML_DH)
    cell = mlstm_cell(heads(q), heads(k), heads(v),
                      i_pre.astype(jnp.float32) + b_i.astype(jnp.float32),
                      f_pre.astype(jnp.float32) + b_f.astype(jnp.float32)).astype(h.dtype)
    cell = rmsnorm(cell.reshape(B, S, ML_HEADS, ML_DH), head_norm.reshape(ML_HEADS, ML_DH)).reshape(B, S, D_MODEL)
    return jnp.einsum('bsf,fd->bsd', jax.nn.sigmoid(o) * cell, w_out)


def squared_relu_mlp(h, w1, w2):
    return jnp.einsum('bsf,fd->bsd', jnp.square(jax.nn.relu(jnp.einsum('bsd,df->bsf', h, w1))), w2)


def setup_inputs(seed: int = 0) -> dict:
    key = jax.random.key(seed)
    ks = jax.random.split(key, 24)
    nrm = lambda k, shape, fan_in: jax.random.normal(k, shape, jnp.float32) * (fan_in ** -0.5)
    gain = lambda k, shape: 1.0 + 0.05 * jax.random.normal(k, shape, jnp.float32)
    offset = jax.random.randint(ks[1], (BATCH, 1), 0, POS_OFFSET_MAX, dtype=jnp.int32)
    return {
        'x': jax.random.normal(ks[0], (BATCH, SEQ, D_MODEL), jnp.float32),
        'positions': offset + jnp.arange(SEQ, dtype=jnp.int32)[None, :],
        'norm_mix_pre': gain(ks[2], (DEPTH, D_MODEL)),
        'norm_mix_post': gain(ks[3], (DEPTH, D_MODEL)),
        'norm_mlp_pre': gain(ks[4], (DEPTH, D_MODEL)),
        'norm_mlp_post': gain(ks[5], (DEPTH, D_MODEL)),
        'hyb_w_in': nrm(ks[6], (N_EVEN, D_MODEL, HYB_IN), D_MODEL),
        'conv_w': nrm(ks[7], (N_EVEN, CONV_WIDTH, CONV_DIM), CONV_WIDTH),
        'mla_q_norm': gain(ks[8], (N_EVEN, MLA_Q_RANK)),
        'mla_kv_norm': gain(ks[9], (N_EVEN, MLA_KV_RANK)),
        'mla_w_uq': nrm(ks[10], (N_EVEN, MLA_Q_RANK, MLA_HEADS * (MLA_NOPE + MLA_ROPE)), MLA_Q_RANK),
        'mla_w_ukv': nrm(ks[11], (N_EVEN, MLA_KV_RANK, MLA_HEADS * (MLA_NOPE + MLA_V)), MLA_KV_RANK),
        'hyb_w_out': nrm(ks[12], (N_EVEN, D_MODEL, D_MODEL), D_MODEL),
        'ml_w_in': nrm(ks[13], (N_ODD, D_MODEL, ML_IN), D_MODEL),
        'ml_b_i': 0.1 * jax.random.normal(ks[14], (N_ODD, ML_HEADS), jnp.float32),
        'ml_b_f': jnp.linspace(3.0, 6.0, ML_HEADS, dtype=jnp.float32)[None, :]
                  + 0.1 * jax.random.normal(ks[15], (N_ODD, ML_HEADS), jnp.float32),
        'ml_head_norm': gain(ks[16], (N_ODD, D_MODEL)),
        'ml_w_out': nrm(ks[17], (N_ODD, D_MODEL, D_MODEL), D_MODEL),
        'mlp_w1': nrm(ks[18], (DEPTH, D_MODEL, D_FF), D_MODEL),
        'mlp_w2': nrm(ks[19], (DEPTH, D_FF, D_MODEL), D_FF),
    }


def reference(x, positions, norm_mix_pre, norm_mix_post, norm_mlp_pre, norm_mlp_post,
              hyb_w_in, conv_w, mla_q_norm, mla_kv_norm, mla_w_uq, mla_w_ukv, hyb_w_out,
              ml_w_in, ml_b_i, ml_b_f, ml_head_norm, ml_w_out, mlp_w1, mlp_w2):
    for l in range(DEPTH):
        h = rmsnorm(x, norm_mix_pre[l])
        if l % 2 == 0:
            e = l // 2
            y = conv_mla_layer(h, positions, hyb_w_in[e], conv_w[e], mla_q_norm[e], mla_kv_norm[e],
                               mla_w_uq[e], mla_w_ukv[e], hyb_w_out[e])
        else:
            o = l // 2
            y = mlstm_layer(h, ml_w_in[o], ml_b_i[o], ml_b_f[o], ml_head_norm[o], ml_w_out[o])
        x = x + rmsnorm(y, norm_mix_post[l])
        h = rmsnorm(x, norm_mlp_pre[l])
        x = x + rmsnorm(squared_relu_mlp(h, mlp_w1[l], mlp_w2[l]), norm_mlp_post[l])
    return x
```

```python
import functools

import jax
import jax.numpy as jnp
from jax import lax
from jax.experimental import pallas as pl
from jax.experimental.pallas import tpu as pltpu

D_MODEL = 1024
EPS = 1e-6
CHUNK = 64
CONV_DIM = 512
CONV_WIDTH = 3
MLA_HEADS = 8
MLA_RANK = 256
MLA_NOPE = 64
MLA_ROPE = 32
MLA_V = 64
ROPE_THETA = 10000.0
ML_HEADS = 4
ML_DH = 256
D_FF = 4096

LANES = 128
HEAD_PAD = LANES
HYB_N = 4 * CONV_DIM + LANES
ML_N = 4 * D_MODEL + LANES
VMEM_LIMIT = 56 * 1024 * 1024

BF16 = jnp.bfloat16
F32 = jnp.float32


def _dot(a, b):
    return jnp.dot(a, b, preferred_element_type=F32)


def _dot_nt(a, b):
    return lax.dot_general(a, b, (((1,), (1,)), ((), ())), preferred_element_type=F32)


def _dot_tn(a, b):
    return lax.dot_general(a, b, (((0,), (0,)), ((), ())), preferred_element_type=F32)


def _rms(x, g):
    return x * lax.rsqrt(jnp.mean(x * x, axis=-1, keepdims=True) + EPS) * g


def _params(*semantics):
    return pltpu.CompilerParams(dimension_semantics=semantics, vmem_limit_bytes=VMEM_LIMIT)


def _rope_table_kernel(pos_ref, inv_ref, sign_ref, c_ref, s_ref):
    ang = pos_ref[...].astype(F32) * inv_ref[...]
    c_ref[...] = jnp.cos(ang)
    s_ref[...] = sign_ref[...] * jnp.sin(ang)


def _rope_tables(positions):
    t = positions.size
    rows = min(t, 2048)
    half = MLA_ROPE // 2
    inv = ROPE_THETA ** (-jnp.arange(half, dtype=F32) / half)
    zeros = jnp.zeros((MLA_NOPE,), F32)
    pad = jnp.zeros((LANES - MLA_NOPE - MLA_ROPE,), F32)
    inv_lane = jnp.concatenate([zeros, inv, inv, pad])[None, :]
    ones = jnp.ones((half,), F32)
    sign_lane = jnp.concatenate([zeros, -ones, ones, pad])[None, :]
    row = pl.BlockSpec((rows, LANES), lambda i: (i, 0))
    const = pl.BlockSpec((1, LANES), lambda i: (0, 0))
    return pl.pallas_call(
        _rope_table_kernel,
        grid=(t // rows,),
        in_specs=[pl.BlockSpec((rows, 1), lambda i: (i, 0)), const, const],
        out_specs=[row, row],
        out_shape=[jax.ShapeDtypeStruct((t, LANES), F32)] * 2,
        compiler_params=_params("parallel"),
        name="rope_tables",
    )(positions.reshape(t, 1), inv_lane, sign_lane)


def _rope(v, c, s):
    lane = lax.broadcasted_iota(jnp.int32, c.shape, 1)
    low = lane < MLA_NOPE + MLA_ROPE // 2
    out = []
    for h in range(v.shape[1] // HEAD_PAD):
        vb = v[:, h * HEAD_PAD:(h + 1) * HEAD_PAD]
        partner = jnp.where(low, pltpu.roll(vb, HEAD_PAD - MLA_ROPE // 2, 1),
                            pltpu.roll(vb, MLA_ROPE // 2, 1))
        out.append(vb * c + partner * s)
    return jnp.concatenate(out, axis=1)


def _hyb_in_kernel(x_ref, g_ref, win_ref, convw_ref, qn_ref, kvn_ref, wuq_ref, wuk_ref,
                   wuv_ref, c_ref, s_ref, ya_ref, q_ref, k_ref, v_ref, u_sc, *, tiles_per_seq):
    tm = x_ref.shape[0]
    xn = _rms(x_ref[...], g_ref[...]).astype(BF16)
    cd = CONV_DIM

    b_gate = _dot(xn, win_ref[:, 0:cd])
    u = _dot(xn, win_ref[:, cd:2 * cd]) * _dot(xn, win_ref[:, 2 * cd:3 * cd])

    @pl.when(pl.program_id(0) % tiles_per_seq == 0)
    def _():
        u_sc[0:8, :] = jnp.zeros((8, cd), F32)

    @pl.when(pl.program_id(0) % tiles_per_seq != 0)
    def _():
        u_sc[0:8, :] = u_sc[tm:tm + 8, :]

    u_sc[8:8 + tm, :] = u
    w = convw_ref[...]
    y = w[2:3, :] * u + w[1:2, :] * u_sc[7:7 + tm, :] + w[0:1, :] * u_sc[6:6 + tm, :]
    ya_ref[...] = (b_gate * y).astype(BF16)

    c, s = c_ref[...], s_ref[...]
    r = MLA_RANK
    cq = _rms(_dot(xn, win_ref[:, 3 * cd:3 * cd + r]), qn_ref[...]).astype(BF16)
    q_ref[...] = _rope(_dot(cq, wuq_ref[...]), c, s).astype(BF16)

    ckv = _rms(_dot(xn, win_ref[:, 3 * cd + r:3 * cd + 2 * r]), kvn_ref[...]).astype(BF16)
    kr = _dot(xn, win_ref[:, 3 * cd + 2 * r:])
    k_pre = _dot(ckv, wuk_ref[...]) + jnp.concatenate([kr] * MLA_HEADS, axis=1)
    k_ref[...] = _rope(k_pre, c, s).astype(BF16)
    v_ref[...] = _dot(ckv, wuv_ref[...]).astype(BF16)


def _hyb_in(x, g, win, convw, qn, kvn, wuq, wuk, wuv, c_tab, s_tab, seq, tm):
    t = x.shape[0]
    hp = MLA_HEADS * HEAD_PAD
    row = lambda n: pl.BlockSpec((tm, n), lambda i: (i, 0))
    full = lambda a: pl.BlockSpec(a.shape, lambda i: (0,) * a.ndim)
    return pl.pallas_call(
        functools.partial(_hyb_in_kernel, tiles_per_seq=seq // tm),
        grid=(t // tm,),
        in_specs=[row(D_MODEL), full(g), full(win), full(convw), full(qn), full(kvn),
                  full(wuq), full(wuk), full(wuv), row(LANES), row(LANES)],
        out_specs=[row(CONV_DIM), row(hp), row(hp), row(MLA_HEADS * MLA_V)],
        out_shape=[jax.ShapeDtypeStruct((t, CONV_DIM), BF16),
                   jax.ShapeDtypeStruct((t, hp), BF16),
                   jax.ShapeDtypeStruct((t, hp), BF16),
                   jax.ShapeDtypeStruct((t, MLA_HEADS * MLA_V), BF16)],
        scratch_shapes=[pltpu.VMEM((tm + 8, CONV_DIM), F32)],
        compiler_params=_params("arbitrary"),
        name="hyb_in",
    )(x, g, win, convw, qn, kvn, wuq, wuk, wuv, c_tab, s_tab)


def _attn_kernel(q_ref, k_ref, v_ref, o_ref, *, blk, scale):
    i = pl.program_id(2)
    lane = lax.broadcasted_iota(jnp.int32, (blk, 2 * MLA_V), 1)
    first = lane < MLA_V
    qs = [q_ref[:, h * HEAD_PAD:(h + 1) * HEAD_PAD] for h in range(2)]

    def step(j, carry, mask):
        ms, ls, acc = carry
        start = pl.multiple_of(j * blk, blk)
        kblk = k_ref[pl.ds(start, blk), :]
        vblk = v_ref[pl.ds(start, blk), :]
        new_m, new_l, alphas, pvs = [], [], [], []
        for h in range(2):
            sc = _dot_nt(qs[h], kblk[:, h * HEAD_PAD:(h + 1) * HEAD_PAD]) * scale
            if mask is not None:
                sc = jnp.where(mask, sc, -jnp.inf)
            m_new = jnp.maximum(ms[h], jnp.max(sc, axis=1, keepdims=True))
            alpha = jnp.exp(ms[h] - m_new)
            p = jnp.exp(sc - m_new)
            new_l.append(alpha * ls[h] + jnp.sum(p, axis=1, keepdims=True))
            new_m.append(m_new)
            alphas.append(alpha)
            pvs.append(_dot(p.astype(BF16), vblk))
        acc = jnp.where(first, alphas[0], alphas[1]) * acc + jnp.where(first, pvs[0], pvs[1])
        return new_m, new_l, acc

    neg = jnp.full((blk, 1), -jnp.inf, F32)
    zero = jnp.zeros((blk, 1), F32)
    carry = ([neg, neg], [zero, zero], jnp.zeros((blk, 2 * MLA_V), F32))
    carry = lax.fori_loop(0, i, lambda j, c: step(j, c, None), carry)
    rows = lax.broadcasted_iota(jnp.int32, (blk, blk), 0) // CHUNK
    cols = lax.broadcasted_iota(jnp.int32, (blk, blk), 1) // CHUNK
    _, ls, acc = step(i, carry, cols <= rows)
    o_ref[...] = (acc * jnp.where(first, 1.0 / ls[0], 1.0 / ls[1])).astype(o_ref.dtype)


def _attention(q, k, v, batch, seq, blk):
    t = q.shape[0]
    nq = seq // blk
    pairs = MLA_HEADS // 2
    scale = float((MLA_NOPE + MLA_ROPE) ** -0.5)
    return pl.pallas_call(
        functools.partial(_attn_kernel, blk=blk, scale=scale),
        grid=(batch, pairs, nq),
        in_specs=[pl.BlockSpec((blk, 2 * HEAD_PAD), lambda b, p, i: (b * nq + i, p)),
                  pl.BlockSpec((seq, 2 * HEAD_PAD), lambda b, p, i: (b, p)),
                  pl.BlockSpec((seq, 2 * MLA_V), lambda b, p, i: (b, p))],
        out_specs=pl.BlockSpec((blk, 2 * MLA_V), lambda b, p, i: (b * nq + i, p)),
        out_shape=jax.ShapeDtypeStruct((t, MLA_HEADS * MLA_V), BF16),
        compiler_params=_params("parallel", "parallel", "arbitrary"),
        name="mla_attention",
    )(q, k, v)


def _ml_in_kernel(x_ref, g_ref, win_ref, bias_ref, q_ref, k_ref, v_ref, o_ref, gate_ref):
    xn = _rms(x_ref[...], g_ref[...]).astype(BF16)
    d = D_MODEL
    q_ref[...] = (_dot(xn, win_ref[:, 0:d]) * (ML_DH ** -0.5)).astype(BF16)
    k_ref[...] = _dot(xn, win_ref[:, d:2 * d]).astype(BF16)
    v_ref[...] = _dot(xn, win_ref[:, 2 * d:3 * d]).astype(BF16)
    o_ref[...] = _dot(xn, win_ref[:, 3 * d:4 * d])
    gate_ref[...] = _dot(xn, win_ref[:, 4 * d:]) + bias_ref[...]


def _ml_in(x, g, win, bias, tm):
    t = x.shape[0]
    row = lambda n: pl.BlockSpec((tm, n), lambda i: (i, 0))
    full = lambda a: pl.BlockSpec(a.shape, lambda i: (0,) * a.ndim)
    act = jax.ShapeDtypeStruct((t, D_MODEL), BF16)
    return pl.pallas_call(
        _ml_in_kernel,
        grid=(t // tm,),
        in_specs=[row(D_MODEL), full(g), full(win), full(bias)],
        out_specs=[row(D_MODEL)] * 4 + [row(LANES)],
        out_shape=[act, act, act, jax.ShapeDtypeStruct((t, D_MODEL), F32),
                   jax.ShapeDtypeStruct((t, LANES), F32)],
        compiler_params=_params("parallel"),
        name="mlstm_in",
    )(x, g, win, bias)


def _log_sigmoid(x):
    return jnp.minimum(x, 0.0) - jnp.log1p(jnp.exp(-jnp.abs(x)))


def _ml_cell_kernel(q_ref, k_ref, v_ref, o_ref, gate_ref, hn_ref, out_ref, c_sc, n_sc, m_sc):
    ln = q_ref.shape[0]

    @pl.when(pl.program_id(1) == 0)
    def _():
        c_sc[...] = jnp.zeros(c_sc.shape, F32)
        n_sc[...] = jnp.zeros(n_sc.shape, F32)
        m_sc[...] = jnp.zeros(m_sc.shape, F32)

    gates = gate_ref[...]
    lsig = _log_sigmoid(gates)
    gates_t = gates.T
    lsig_t = lsig.T
    t_idx = lax.broadcasted_iota(jnp.int32, (ln, ln), 0)
    s_idx = lax.broadcasted_iota(jnp.int32, (ln, ln), 1)
    causal = s_idx <= t_idx

    for h in range(ML_HEADS):
        sl = slice(h * ML_DH, (h + 1) * ML_DH)
        li_col = gates[:, h:h + 1]
        lf_col = lsig[:, ML_HEADS + h:ML_HEADS + h + 1]
        li_row = gates_t[h:h + 1, :]
        lf_row = lsig_t[ML_HEADS + h:ML_HEADS + h + 1, :]
        b_col = jnp.sum(jnp.where(causal, lf_row, 0.0), axis=1, keepdims=True)
        b_row = jnp.sum(jnp.where(t_idx <= s_idx, lf_col, 0.0), axis=0, keepdims=True)
        d_log = jnp.where(causal, b_col - b_row + li_row, -jnp.inf)
        m_intra = jnp.max(d_log, axis=1, keepdims=True)
        m_prev = m_sc[h][:, 0:1]
        m_t = jnp.maximum(b_col + m_prev, m_intra)
        inter = jnp.exp(b_col + m_prev - m_t)
        qh, kh, vh = q_ref[:, sl], k_ref[:, sl], v_ref[:, sl]
        scores = _dot_nt(qh, kh) * jnp.exp(d_log - m_t)
        c_prev = c_sc[h]
        n_prev = n_sc[h]
        num = inter * _dot(qh, c_prev.astype(BF16)) + _dot(scores.astype(BF16), vh)
        den = (inter * jnp.sum(qh.astype(F32) * n_prev, axis=1, keepdims=True)
               + jnp.sum(scores, axis=1, keepdims=True))
        cell = num * (1.0 / jnp.maximum(jnp.abs(den), jnp.exp(-m_t)))
        normed = _rms(cell, hn_ref[:, sl])
        out_ref[:, sl] = (jax.nn.sigmoid(o_ref[:, sl]) * normed).astype(out_ref.dtype)

        b_last = b_col[ln - 1:ln, :]
        a_col = b_last - b_col + li_col
        a_max = jnp.max(a_col, axis=0, keepdims=True)
        kw = kh.astype(F32) * jnp.exp(a_col - a_max)
        c_loc = _dot_tn(kw.astype(BF16), vh)
        n_loc = jnp.sum(kw, axis=0, keepdims=True)
        m_new = jnp.maximum(b_last + m_prev, a_max)
        sp = jnp.exp(b_last + m_prev - m_new)
        sn = jnp.exp(a_max - m_new)
        c_sc[h] = sp * c_prev + sn * c_loc
        n_sc[h] = sp * n_prev + sn * n_loc
        m_sc[h] = jnp.broadcast_to(m_new, (1, LANES))


def _ml_cell(q, k, v, o, gates, head_norm, batch, seq, ln):
    t = q.shape[0]
    nc = seq // ln
    row = lambda n: pl.BlockSpec((ln, n), lambda b, c: (b * nc + c, 0))
    return pl.pallas_call(
        _ml_cell_kernel,
        grid=(batch, nc),
        in_specs=[row(D_MODEL)] * 4 + [row(LANES), pl.BlockSpec((1, D_MODEL), lambda b, c: (0, 0))],
        out_specs=row(D_MODEL),
        out_shape=jax.ShapeDtypeStruct((t, D_MODEL), BF16),
        scratch_shapes=[pltpu.VMEM((ML_HEADS, ML_DH, ML_DH), F32),
                        pltpu.VMEM((ML_HEADS, 1, ML_DH), F32),
                        pltpu.VMEM((ML_HEADS, 1, LANES), F32)],
        compiler_params=_params("parallel", "arbitrary"),
        name="mlstm_cell",
    )(q, k, v, o, gates, head_norm)


def _out_mlp_kernel(*refs, n_parts):
    parts = refs[:n_parts]
    (wout_ref, x_ref, gpost_ref, gpre_ref, gmlp_ref, w1_ref, w2_ref,
     out_ref, x1_sc, hn_sc, acc_sc) = refs[n_parts:]
    j = pl.program_id(1)

    @pl.when(j == 0)
    def _():
        y, r0 = None, 0
        for p in parts:
            kp = p.shape[1]
            d = _dot(p[...], wout_ref[r0:r0 + kp, :])
            y = d if y is None else y + d
            r0 += kp
        x1 = x_ref[...] + _rms(y, gpost_ref[...])
        x1_sc[...] = x1
        hn_sc[...] = _rms(x1, gpre_ref[...]).astype(BF16)

    a = jnp.square(jnp.maximum(_dot(hn_sc[...], w1_ref[...]), 0.0)).astype(BF16)
    contrib = _dot(a, w2_ref[...])

    @pl.when(j == 0)
    def _():
        acc_sc[...] = contrib

    @pl.when(j != 0)
    def _():
        acc_sc[...] += contrib

    @pl.when(j == pl.num_programs(1) - 1)
    def _():
        out_ref[...] = x1_sc[...] + _rms(acc_sc[...], gmlp_ref[...])


def _out_mlp(parts, wout, x, gpost, gpre, gmlp, w1, w2, tm, tf):
    t = x.shape[0]
    row = lambda n: pl.BlockSpec((tm, n), lambda i, j: (i, 0))
    vec = pl.BlockSpec((1, D_MODEL), lambda i, j: (0, 0))
    return pl.pallas_call(
        functools.partial(_out_mlp_kernel, n_parts=len(parts)),
        grid=(t // tm, D_FF // tf),
        in_specs=[row(p.shape[1]) for p in parts]
        + [pl.BlockSpec((D_MODEL, D_MODEL), lambda i, j: (0, 0)), row(D_MODEL), vec, vec, vec,
           pl.BlockSpec((D_MODEL, tf), lambda i, j: (0, j)),
           pl.BlockSpec((tf, D_MODEL), lambda i, j: (j, 0))],
        out_specs=row(D_MODEL),
        out_shape=jax.ShapeDtypeStruct((t, D_MODEL), F32),
        scratch_shapes=[pltpu.VMEM((tm, D_MODEL), F32), pltpu.VMEM((tm, D_MODEL), BF16),
                        pltpu.VMEM((tm, D_MODEL), F32)],
        compiler_params=_params("parallel", "arbitrary"),
        name="out_mlp",
    )(*parts, wout, x, gpost, gpre, gmlp, w1, w2)


def _pad_heads(w, width):
    r = w.shape[0]
    w = w.reshape(r, MLA_HEADS, width)
    w = jnp.pad(w, ((0, 0), (0, 0), (0, HEAD_PAD - width)))
    return w.reshape(r, MLA_HEADS * HEAD_PAD)


def _hyb_weights(w_in, w_uq, w_ukv):
    main = w_in[:, :3 * CONV_DIM + 2 * MLA_RANK]
    k_r = jnp.pad(w_in[:, 3 * CONV_DIM + 2 * MLA_RANK:], ((0, 0), (MLA_NOPE, LANES - MLA_NOPE - MLA_ROPE)))
    win = jnp.concatenate([main, k_r], axis=1).astype(BF16)
    wuq = _pad_heads(w_uq, MLA_NOPE + MLA_ROPE).astype(BF16)
    ukv = w_ukv.reshape(MLA_RANK, MLA_HEADS, MLA_NOPE + MLA_V)
    wuk = _pad_heads(ukv[:, :, :MLA_NOPE].reshape(MLA_RANK, -1), MLA_NOPE).astype(BF16)
    wuv = ukv[:, :, MLA_NOPE:].reshape(MLA_RANK, MLA_HEADS * MLA_V).astype(BF16)
    return win, wuq, wuk, wuv


def kernel(x, positions, norm_mix_pre, norm_mix_post, norm_mlp_pre, norm_mlp_post, hyb_w_in, conv_w, mla_q_norm, mla_kv_norm, mla_w_uq, mla_w_ukv, hyb_w_out, ml_w_in, ml_b_i, ml_b_f, ml_head_norm, ml_w_out, mlp_w1, mlp_w2):
    batch, seq, d = x.shape
    t = batch * seq
    depth = norm_mix_pre.shape[0]
    tm_in = min(512, seq)
    tm_mlp = min(1024, t)
    tf = 1024
    blk = min(256, seq)

    xt = x.reshape(t, d)
    c_tab, s_tab = _rope_tables(positions)
    vec = lambda a: a.reshape(1, -1)

    for l in range(depth):
        e = l // 2
        if l % 2 == 0:
            win, wuq, wuk, wuv = _hyb_weights(hyb_w_in[e], mla_w_uq[e], mla_w_ukv[e])
            y_a, q, k, v = _hyb_in(xt, vec(norm_mix_pre[l]), win, conv_w[e], vec(mla_q_norm[e]),
                                   vec(mla_kv_norm[e]), wuq, wuk, wuv, c_tab, s_tab, seq, tm_in)
            y_b = _attention(q, k, v, batch, seq, blk)
            parts, wout = [y_a, y_b], hyb_w_out[e].astype(BF16)
        else:
            win = jnp.pad(ml_w_in[e], ((0, 0), (0, ML_N - ml_w_in.shape[2]))).astype(BF16)
            bias = jnp.pad(jnp.concatenate([ml_b_i[e], ml_b_f[e]]), (0, LANES - 2 * ML_HEADS))[None, :]
            q, k, v, o, gates = _ml_in(xt, vec(norm_mix_pre[l]), win, bias, tm_in)
            cell = _ml_cell(q, k, v, o, gates, vec(ml_head_norm[e]), batch, seq, blk)
            parts, wout = [cell], ml_w_out[e].astype(BF16)
        xt = _out_mlp(parts, wout, xt, vec(norm_mix_post[l]), vec(norm_mlp_pre[l]),
                      vec(norm_mlp_post[l]), mlp_w1[l].astype(BF16), mlp_w2[l].astype(BF16), tm_mlp, tf)
    return xt.reshape(batch, seq, d)
```

```python
import functools

import jax
import jax.numpy as jnp
from jax import lax
from jax.experimental import pallas as pl
from jax.experimental.pallas import tpu as pltpu

D_MODEL = 1024
EPS = 1e-6
CHUNK = 64
CONV_DIM = 512
CONV_WIDTH = 3
MLA_HEADS = 8
MLA_RANK = 256
MLA_NOPE = 64
MLA_ROPE = 32
MLA_V = 64
ROPE_THETA = 10000.0
ML_HEADS = 4
ML_DH = 256
D_FF = 4096

LANES = 128
HEAD_PAD = LANES
HYB_N = 4 * CONV_DIM + LANES
ML_N = 4 * D_MODEL + LANES
VMEM_LIMIT = 56 * 1024 * 1024

BF16 = jnp.bfloat16
F32 = jnp.float32


def _dot(a, b):
    return jnp.dot(a, b, preferred_element_type=F32)


def _dot_nt(a, b):
    return lax.dot_general(a, b, (((1,), (1,)), ((), ())), preferred_element_type=F32)


def _dot_tn(a, b):
    return lax.dot_general(a, b, (((0,), (0,)), ((), ())), preferred_element_type=F32)


def _rms(x, g):
    return x * lax.rsqrt(jnp.mean(x * x, axis=-1, keepdims=True) + EPS) * g


def _params(*semantics):
    return pltpu.CompilerParams(dimension_semantics=semantics, vmem_limit_bytes=VMEM_LIMIT)


def _rope_table_kernel(pos_ref, inv_ref, sign_ref, c_ref, s_ref):
    ang = pos_ref[...].astype(F32) * inv_ref[...]
    c_ref[...] = jnp.cos(ang)
    s_ref[...] = sign_ref[...] * jnp.sin(ang)


def _rope_tables(positions):
    t = positions.size
    rows = min(t, 2048)
    half = MLA_ROPE // 2
    inv = ROPE_THETA ** (-jnp.arange(half, dtype=F32) / half)
    zeros = jnp.zeros((MLA_NOPE,), F32)
    pad = jnp.zeros((LANES - MLA_NOPE - MLA_ROPE,), F32)
    inv_lane = jnp.concatenate([zeros, inv, inv, pad])[None, :]
    ones = jnp.ones((half,), F32)
    sign_lane = jnp.concatenate([zeros, -ones, ones, pad])[None, :]
    row = pl.BlockSpec((rows, LANES), lambda i: (i, 0))
    const = pl.BlockSpec((1, LANES), lambda i: (0, 0))
    return pl.pallas_call(
        _rope_table_kernel,
        grid=(t // rows,),
        in_specs=[pl.BlockSpec((rows, 1), lambda i: (i, 0)), const, const],
        out_specs=[row, row],
        out_shape=[jax.ShapeDtypeStruct((t, LANES), F32)] * 2,
        compiler_params=_params("parallel"),
        name="rope_tables",
    )(positions.reshape(t, 1), inv_lane, sign_lane)


def _rope(v, c, s):
    lane = lax.broadcasted_iota(jnp.int32, c.shape, 1)
    low = lane < MLA_NOPE + MLA_ROPE // 2
    out = []
    for h in range(v.shape[1] // HEAD_PAD):
        vb = v[:, h * HEAD_PAD:(h + 1) * HEAD_PAD]
        partner = jnp.where(low, pltpu.roll(vb, HEAD_PAD - MLA_ROPE // 2, 1),
                            pltpu.roll(vb, MLA_ROPE // 2, 1))
        out.append(vb * c + partner * s)
    return jnp.concatenate(out, axis=1)


def _hyb_in_kernel(x_ref, g_ref, win_ref, convw_ref, qn_ref, kvn_ref, wuq_ref, wuk_ref,
                   wuv_ref, c_ref, s_ref, ya_ref, q_ref, k_ref, v_ref, u_sc, *, tiles_per_seq):
    tm = x_ref.shape[0]
    xn = _rms(x_ref[...], g_ref[...]).astype(BF16)
    cd = CONV_DIM

    b_gate = _dot(xn, win_ref[:, 0:cd])
    u = _dot(xn, win_ref[:, cd:2 * cd]) * _dot(xn, win_ref[:, 2 * cd:3 * cd])

    @pl.when(pl.program_id(0) % tiles_per_seq == 0)
    def _():
        u_sc[0:8, :] = jnp.zeros((8, cd), F32)

    @pl.when(pl.program_id(0) % tiles_per_seq != 0)
    def _():
        u_sc[0:8, :] = u_sc[tm:tm + 8, :]

    u_sc[8:8 + tm, :] = u
    w = convw_ref[...]
    y = w[2:3, :] * u + w[1:2, :] * u_sc[7:7 + tm, :] + w[0:1, :] * u_sc[6:6 + tm, :]
    ya_ref[...] = (b_gate * y).astype(BF16)

    c, s = c_ref[...], s_ref[...]
    r = MLA_RANK
    cq = _rms(_dot(xn, win_ref[:, 3 * cd:3 * cd + r]), qn_ref[...]).astype(BF16)
    q_ref[...] = _rope(_dot(cq, wuq_ref[...]), c, s).astype(BF16)

    ckv = _rms(_dot(xn, win_ref[:, 3 * cd + r:3 * cd + 2 * r]), kvn_ref[...]).astype(BF16)
    kr = _dot(xn, win_ref[:, 3 * cd + 2 * r:])
    k_pre = _dot(ckv, wuk_ref[...]) + jnp.concatenate([kr] * MLA_HEADS, axis=1)
    k_ref[...] = _rope(k_pre, c, s).astype(BF16)
    v_ref[...] = _dot(ckv, wuv_ref[...]).astype(BF16)


def _hyb_in(x, g, win, convw, qn, kvn, wuq, wuk, wuv, c_tab, s_tab, seq, tm):
    t = x.shape[0]
    hp = MLA_HEADS * HEAD_PAD
    row = lambda n: pl.BlockSpec((tm, n), lambda i: (i, 0))
    full = lambda a: pl.BlockSpec(a.shape, lambda i: (0,) * a.ndim)
    return pl.pallas_call(
        functools.partial(_hyb_in_kernel, tiles_per_seq=seq // tm),
        grid=(t // tm,),
        in_specs=[row(D_MODEL), full(g), full(win), full(convw), full(qn), full(kvn),
                  full(wuq), full(wuk), full(wuv), row(LANES), row(LANES)],
        out_specs=[row(CONV_DIM), row(hp), row(hp), row(MLA_HEADS * MLA_V)],
        out_shape=[jax.ShapeDtypeStruct((t, CONV_DIM), BF16),
                   jax.ShapeDtypeStruct((t, hp), BF16),
                   jax.ShapeDtypeStruct((t, hp), BF16),
                   jax.ShapeDtypeStruct((t, MLA_HEADS * MLA_V), BF16)],
        scratch_shapes=[pltpu.VMEM((tm + 8, CONV_DIM), F32)],
        compiler_params=_params("arbitrary"),
        name="hyb_in",
    )(x, g, win, convw, qn, kvn, wuq, wuk, wuv, c_tab, s_tab)


def _attn_kernel(q_ref, k_ref, v_ref, o_ref, *, blk, scale):
    seq = q_ref.shape[0]
    lane = lax.broadcasted_iota(jnp.int32, (blk, 2 * MLA_V), 1)
    first = lane < MLA_V
    rows = lax.broadcasted_iota(jnp.int32, (blk, blk), 0) // CHUNK
    cols = lax.broadcasted_iota(jnp.int32, (blk, blk), 1) // CHUNK
    visible = cols <= rows

    for i in range(seq // blk):
        q0 = i * blk
        outs = []
        for h in range(2):
            hs = slice(h * HEAD_PAD, (h + 1) * HEAD_PAD)
            qh = q_ref[q0:q0 + blk, hs]
            s_diag = jnp.where(visible, _dot_nt(qh, k_ref[q0:q0 + blk, hs]), -jnp.inf)
            m = jnp.max(s_diag, axis=1, keepdims=True)
            if i > 0:
                s_off = _dot_nt(qh, k_ref[0:q0, hs])
                m = jnp.maximum(m, jnp.max(s_off, axis=1, keepdims=True))
            p_diag = jnp.exp((s_diag - m) * scale)
            den = jnp.sum(p_diag, axis=1, keepdims=True)
            pv = _dot(p_diag.astype(BF16), v_ref[q0:q0 + blk, :])
            if i > 0:
                p_off = jnp.exp((s_off - m) * scale)
                den = den + jnp.sum(p_off, axis=1, keepdims=True)
                pv = pv + _dot(p_off.astype(BF16), v_ref[0:q0, :])
            outs.append(pv * (1.0 / den))
        o_ref[q0:q0 + blk, :] = jnp.where(first, outs[0], outs[1]).astype(o_ref.dtype)


def _attention(q, k, v, batch, seq, blk):
    t = q.shape[0]
    pairs = MLA_HEADS // 2
    scale = float((MLA_NOPE + MLA_ROPE) ** -0.5)
    spec = lambda n: pl.BlockSpec((seq, n), lambda b, p: (b, p))
    return pl.pallas_call(
        functools.partial(_attn_kernel, blk=blk, scale=scale),
        grid=(batch, pairs),
        in_specs=[spec(2 * HEAD_PAD), spec(2 * HEAD_PAD), spec(2 * MLA_V)],
        out_specs=spec(2 * MLA_V),
        out_shape=jax.ShapeDtypeStruct((t, MLA_HEADS * MLA_V), BF16),
        compiler_params=_params("parallel", "parallel"),
        name="mla_attention",
    )(q, k, v)


def _ml_in_kernel(x_ref, g_ref, win_ref, bias_ref, q_ref, k_ref, v_ref, o_ref, gate_ref):
    xn = _rms(x_ref[...], g_ref[...]).astype(BF16)
    d = D_MODEL
    q_ref[...] = (_dot(xn, win_ref[:, 0:d]) * (ML_DH ** -0.5)).astype(BF16)
    k_ref[...] = _dot(xn, win_ref[:, d:2 * d]).astype(BF16)
    v_ref[...] = _dot(xn, win_ref[:, 2 * d:3 * d]).astype(BF16)
    o_ref[...] = _dot(xn, win_ref[:, 3 * d:4 * d])
    gate_ref[...] = _dot(xn, win_ref[:, 4 * d:]) + bias_ref[...]


def _ml_in(x, g, win, bias, tm):
    t = x.shape[0]
    row = lambda n: pl.BlockSpec((tm, n), lambda i: (i, 0))
    full = lambda a: pl.BlockSpec(a.shape, lambda i: (0,) * a.ndim)
    act = jax.ShapeDtypeStruct((t, D_MODEL), BF16)
    return pl.pallas_call(
        _ml_in_kernel,
        grid=(t // tm,),
        in_specs=[row(D_MODEL), full(g), full(win), full(bias)],
        out_specs=[row(D_MODEL)] * 4 + [row(LANES)],
        out_shape=[act, act, act, jax.ShapeDtypeStruct((t, D_MODEL), F32),
                   jax.ShapeDtypeStruct((t, LANES), F32)],
        compiler_params=_params("parallel"),
        name="mlstm_in",
    )(x, g, win, bias)


def _log_sigmoid(x):
    return jnp.minimum(x, 0.0) - jnp.log1p(jnp.exp(-jnp.abs(x)))


def _ml_cell_kernel(q_ref, k_ref, v_ref, o_ref, gate_ref, hn_ref, out_ref, c_sc, n_sc, m_sc):
    ln = q_ref.shape[0]

    @pl.when(pl.program_id(1) == 0)
    def _():
        c_sc[...] = jnp.zeros(c_sc.shape, F32)
        n_sc[...] = jnp.zeros(n_sc.shape, F32)
        m_sc[...] = jnp.zeros(m_sc.shape, F32)

    gates = gate_ref[...]
    lsig = _log_sigmoid(gates)
    gates_t = gates.T
    lsig_t = lsig.T
    t_idx = lax.broadcasted_iota(jnp.int32, (ln, ln), 0)
    s_idx = lax.broadcasted_iota(jnp.int32, (ln, ln), 1)
    causal = s_idx <= t_idx

    for h in range(ML_HEADS):
        sl = slice(h * ML_DH, (h + 1) * ML_DH)
        li_col = gates[:, h:h + 1]
        lf_col = lsig[:, ML_HEADS + h:ML_HEADS + h + 1]
        li_row = gates_t[h:h + 1, :]
        lf_row = lsig_t[ML_HEADS + h:ML_HEADS + h + 1, :]
        b_col = jnp.sum(jnp.where(causal, lf_row, 0.0), axis=1, keepdims=True)
        b_row = jnp.sum(jnp.where(t_idx <= s_idx, lf_col, 0.0), axis=0, keepdims=True)
        d_log = jnp.where(causal, b_col - b_row + li_row, -jnp.inf)
        m_intra = jnp.max(d_log, axis=1, keepdims=True)
        m_prev = m_sc[h][:, 0:1]
        m_t = jnp.maximum(b_col + m_prev, m_intra)
        inter = jnp.exp(b_col + m_prev - m_t)
        qh, kh, vh = q_ref[:, sl], k_ref[:, sl], v_ref[:, sl]
        scores = _dot_nt(qh, kh) * jnp.exp(d_log - m_t)
        c_prev = c_sc[h]
        n_prev = n_sc[h]
        num = inter * _dot(qh, c_prev.astype(BF16)) + _dot(scores.astype(BF16), vh)
        den = (inter * jnp.sum(qh.astype(F32) * n_prev, axis=1, keepdims=True)
               + jnp.sum(scores, axis=1, keepdims=True))
        cell = num * (1.0 / jnp.maximum(jnp.abs(den), jnp.exp(-m_t)))
        normed = _rms(cell, hn_ref[:, sl])
        out_ref[:, sl] = (jax.nn.sigmoid(o_ref[:, sl]) * normed).astype(out_ref.dtype)

        b_last = b_col[ln - 1:ln, :]
        a_col = b_last - b_col + li_col
        a_max = jnp.max(a_col, axis=0, keepdims=True)
        kw = kh.astype(F32) * jnp.exp(a_col - a_max)
        c_loc = _dot_tn(kw.astype(BF16), vh)
        n_loc = jnp.sum(kw, axis=0, keepdims=True)
        m_new = jnp.maximum(b_last + m_prev, a_max)
        sp = jnp.exp(b_last + m_prev - m_new)
        sn = jnp.exp(a_max - m_new)
        c_sc[h] = sp * c_prev + sn * c_loc
        n_sc[h] = sp * n_prev + sn * n_loc
        m_sc[h] = jnp.broadcast_to(m_new, (1, LANES))


def _ml_cell(q, k, v, o, gates, head_norm, batch, seq, ln):
    t = q.shape[0]
    nc = seq // ln
    row = lambda n: pl.BlockSpec((ln, n), lambda b, c: (b * nc + c, 0))
    return pl.pallas_call(
        _ml_cell_kernel,
        grid=(batch, nc),
        in_specs=[row(D_MODEL)] * 4 + [row(LANES), pl.BlockSpec((1, D_MODEL), lambda b, c: (0, 0))],
        out_specs=row(D_MODEL),
        out_shape=jax.ShapeDtypeStruct((t, D_MODEL), BF16),
        scratch_shapes=[pltpu.VMEM((ML_HEADS, ML_DH, ML_DH), F32),
                        pltpu.VMEM((ML_HEADS, 1, ML_DH), F32),
                        pltpu.VMEM((ML_HEADS, 1, LANES), F32)],
        compiler_params=_params("parallel", "arbitrary"),
        name="mlstm_cell",
    )(q, k, v, o, gates, head_norm)


def _out_mlp_kernel(*refs, n_parts):
    parts = refs[:n_parts]
    (wout_ref, x_ref, gpost_ref, gpre_ref, gmlp_ref, w1_ref, w2_ref,
     out_ref, x1_sc, hn_sc, acc_sc) = refs[n_parts:]
    j = pl.program_id(1)

    @pl.when(j == 0)
    def _():
        y, r0 = None, 0
        for p in parts:
            kp = p.shape[1]
            d = _dot(p[...], wout_ref[r0:r0 + kp, :])
            y = d if y is None else y + d
            r0 += kp
        x1 = x_ref[...] + _rms(y, gpost_ref[...])
        x1_sc[...] = x1
        hn_sc[...] = _rms(x1, gpre_ref[...]).astype(BF16)

    a = jnp.square(jnp.maximum(_dot(hn_sc[...], w1_ref[...]), 0.0)).astype(BF16)
    contrib = _dot(a, w2_ref[...])

    @pl.when(j == 0)
    def _():
        acc_sc[...] = contrib

    @pl.when(j != 0)
    def _():
        acc_sc[...] += contrib

    @pl.when(j == pl.num_programs(1) - 1)
    def _():
        out_ref[...] = x1_sc[...] + _rms(acc_sc[...], gmlp_ref[...])


def _out_mlp(parts, wout, x, gpost, gpre, gmlp, w1, w2, tm, tf):
    t = x.shape[0]
    row = lambda n: pl.BlockSpec((tm, n), lambda i, j: (i, 0))
    vec = pl.BlockSpec((1, D_MODEL), lambda i, j: (0, 0))
    return pl.pallas_call(
        functools.partial(_out_mlp_kernel, n_parts=len(parts)),
        grid=(t // tm, D_FF // tf),
        in_specs=[row(p.shape[1]) for p in parts]
        + [pl.BlockSpec((D_MODEL, D_MODEL), lambda i, j: (0, 0)), row(D_MODEL), vec, vec, vec,
           pl.BlockSpec((D_MODEL, tf), lambda i, j: (0, j)),
           pl.BlockSpec((tf, D_MODEL), lambda i, j: (j, 0))],
        out_specs=row(D_MODEL),
        out_shape=jax.ShapeDtypeStruct((t, D_MODEL), F32),
        scratch_shapes=[pltpu.VMEM((tm, D_MODEL), F32), pltpu.VMEM((tm, D_MODEL), BF16),
                        pltpu.VMEM((tm, D_MODEL), F32)],
        compiler_params=_params("parallel", "arbitrary"),
        name="out_mlp",
    )(*parts, wout, x, gpost, gpre, gmlp, w1, w2)


def _pad_heads(w, width):
    r = w.shape[0]
    w = w.reshape(r, MLA_HEADS, width)
    w = jnp.pad(w, ((0, 0), (0, 0), (0, HEAD_PAD - width)))
    return w.reshape(r, MLA_HEADS * HEAD_PAD)


def _hyb_weights(w_in, w_uq, w_ukv):
    main = w_in[:, :3 * CONV_DIM + 2 * MLA_RANK]
    k_r = jnp.pad(w_in[:, 3 * CONV_DIM + 2 * MLA_RANK:], ((0, 0), (MLA_NOPE, LANES - MLA_NOPE - MLA_ROPE)))
    win = jnp.concatenate([main, k_r], axis=1).astype(BF16)
    wuq = _pad_heads(w_uq, MLA_NOPE + MLA_ROPE).astype(BF16)
    ukv = w_ukv.reshape(MLA_RANK, MLA_HEADS, MLA_NOPE + MLA_V)
    wuk = _pad_heads(ukv[:, :, :MLA_NOPE].reshape(MLA_RANK, -1), MLA_NOPE).astype(BF16)
    wuv = ukv[:, :, MLA_NOPE:].reshape(MLA_RANK, MLA_HEADS * MLA_V).astype(BF16)
    return win, wuq, wuk, wuv


def kernel(x, positions, norm_mix_pre, norm_mix_post, norm_mlp_pre, norm_mlp_post, hyb_w_in, conv_w, mla_q_norm, mla_kv_norm, mla_w_uq, mla_w_ukv, hyb_w_out, ml_w_in, ml_b_i, ml_b_f, ml_head_norm, ml_w_out, mlp_w1, mlp_w2):
    batch, seq, d = x.shape
    t = batch * seq
    depth = norm_mix_pre.shape[0]
    tm_in = min(512, seq)
    tm_mlp = min(1024, t)
    tf = 1024
    blk = min(256, seq)

    xt = x.reshape(t, d)
    c_tab, s_tab = _rope_tables(positions)
    vec = lambda a: a.reshape(1, -1)

    for l in range(depth):
        e = l // 2
        if l % 2 == 0:
            win, wuq, wuk, wuv = _hyb_weights(hyb_w_in[e], mla_w_uq[e], mla_w_ukv[e])
            y_a, q, k, v = _hyb_in(xt, vec(norm_mix_pre[l]), win, conv_w[e], vec(mla_q_norm[e]),
                                   vec(mla_kv_norm[e]), wuq, wuk, wuv, c_tab, s_tab, seq, tm_in)
            y_b = _attention(q, k, v, batch, seq, blk)
            parts, wout = [y_a, y_b], hyb_w_out[e].astype(BF16)
        else:
            win = jnp.pad(ml_w_in[e], ((0, 0), (0, ML_N - ml_w_in.shape[2]))).astype(BF16)
            bias = jnp.pad(jnp.concatenate([ml_b_i[e], ml_b_f[e]]), (0, LANES - 2 * ML_HEADS))[None, :]
            q, k, v, o, gates = _ml_in(xt, vec(norm_mix_pre[l]), win, bias, tm_in)
            cell = _ml_cell(q, k, v, o, gates, vec(ml_head_norm[e]), batch, seq, blk)
            parts, wout = [cell], ml_w_out[e].astype(BF16)
        xt = _out_mlp(parts, wout, xt, vec(norm_mix_post[l]), vec(norm_mlp_pre[l]),
                      vec(norm_mlp_post[l]), mlp_w1[l].astype(BF16), mlp_w2[l].astype(BF16), tm_mlp, tf)
    return xt.reshape(batch, seq, d)
```

```python
import functools

import jax
import jax.numpy as jnp
from jax import lax
from jax.experimental import pallas as pl
from jax.experimental.pallas import tpu as pltpu

D_MODEL = 1024
EPS = 1e-6
CHUNK = 64
CONV_DIM = 512
CONV_WIDTH = 3
MLA_HEADS = 8
MLA_RANK = 256
MLA_NOPE = 64
MLA_ROPE = 32
MLA_V = 64
ROPE_THETA = 10000.0
ML_HEADS = 4
ML_DH = 256
D_FF = 4096

LANES = 128
HEAD_PAD = LANES
HYB_N = 4 * CONV_DIM + LANES
ML_N = 4 * D_MODEL + LANES
VMEM_LIMIT = 56 * 1024 * 1024

BF16 = jnp.bfloat16
F32 = jnp.float32


def _dot(a, b):
    return jnp.dot(a, b, preferred_element_type=F32)


def _dot_nt(a, b):
    return lax.dot_general(a, b, (((1,), (1,)), ((), ())), preferred_element_type=F32)


def _dot_tn(a, b):
    return lax.dot_general(a, b, (((0,), (0,)), ((), ())), preferred_element_type=F32)


def _rms(x, g):
    return x * lax.rsqrt(jnp.mean(x * x, axis=-1, keepdims=True) + EPS) * g


def _params(*semantics):
    return pltpu.CompilerParams(dimension_semantics=semantics, vmem_limit_bytes=VMEM_LIMIT)


def _resident(a, layer=None):
    if layer is None:
        return pl.BlockSpec(a.shape, lambda *_: (0,) * a.ndim, pipeline_mode=pl.Buffered(1))
    return pl.BlockSpec((None,) + a.shape[1:], lambda *_: (layer,) + (0,) * (a.ndim - 1),
                        pipeline_mode=pl.Buffered(1))


def _rope_table_kernel(pos_ref, inv_ref, sign_ref, c_ref, s_ref):
    ang = pos_ref[...].astype(F32) * inv_ref[...]
    c_ref[...] = jnp.cos(ang)
    s_ref[...] = sign_ref[...] * jnp.sin(ang)


def _rope_tables(positions):
    t = positions.size
    rows = min(t, 2048)
    half = MLA_ROPE // 2
    inv = ROPE_THETA ** (-jnp.arange(half, dtype=F32) / half)
    zeros = jnp.zeros((MLA_NOPE,), F32)
    pad = jnp.zeros((LANES - MLA_NOPE - MLA_ROPE,), F32)
    inv_lane = jnp.concatenate([zeros, inv, inv, pad])[None, :]
    ones = jnp.ones((half,), F32)
    sign_lane = jnp.concatenate([zeros, -ones, ones, pad])[None, :]
    row = pl.BlockSpec((rows, LANES), lambda i: (i, 0))
    const = pl.BlockSpec((1, LANES), lambda i: (0, 0))
    return pl.pallas_call(
        _rope_table_kernel,
        grid=(t // rows,),
        in_specs=[pl.BlockSpec((rows, 1), lambda i: (i, 0)), const, const],
        out_specs=[row, row],
        out_shape=[jax.ShapeDtypeStruct((t, LANES), F32)] * 2,
        compiler_params=_params("parallel"),
        name="rope_tables",
    )(positions.reshape(t, 1), inv_lane, sign_lane)


def _rope(v, c, s):
    lane = lax.broadcasted_iota(jnp.int32, c.shape, 1)
    low = lane < MLA_NOPE + MLA_ROPE // 2
    out = []
    for h in range(v.shape[1] // HEAD_PAD):
        vb = v[:, h * HEAD_PAD:(h + 1) * HEAD_PAD]
        partner = jnp.where(low, pltpu.roll(vb, HEAD_PAD - MLA_ROPE // 2, 1),
                            pltpu.roll(vb, MLA_ROPE // 2, 1))
        out.append(vb * c + partner * s)
    return jnp.concatenate(out, axis=1)


def _hyb_in_kernel(x_ref, g_ref, win_ref, convw_ref, qn_ref, kvn_ref, wuq_ref, wuk_ref,
                   wuv_ref, c_ref, s_ref, ya_ref, q_ref, k_ref, v_ref, u_sc, *, tiles_per_seq,
                   n_chains):
    tm = x_ref.shape[0]
    rc = tm // n_chains
    cd, r = CONV_DIM, MLA_RANK

    @pl.when(pl.program_id(0) % tiles_per_seq == 0)
    def _():
        u_sc[0:8, :] = jnp.zeros((8, cd), F32)

    @pl.when(pl.program_id(0) % tiles_per_seq != 0)
    def _():
        u_sc[0:8, :] = u_sc[tm:tm + 8, :]

    chains = [slice(c * rc, (c + 1) * rc) for c in range(n_chains)]
    xns = [_rms(x_ref[rs, :], g_ref[...]).astype(BF16) for rs in chains]
    w = convw_ref[...]
    lows = []
    for rs, xn in zip(chains, xns):
        b_gate = _dot(xn, win_ref[:, 0:cd])
        u = _dot(xn, win_ref[:, cd:2 * cd]) * _dot(xn, win_ref[:, 2 * cd:3 * cd])
        u_sc[8 + rs.start:8 + rs.stop, :] = u
        y = (w[2:3, :] * u + w[1:2, :] * u_sc[7 + rs.start:7 + rs.stop, :]
             + w[0:1, :] * u_sc[6 + rs.start:6 + rs.stop, :])
        ya_ref[rs, :] = (b_gate * y).astype(BF16)
        cq = _rms(_dot(xn, win_ref[:, 3 * cd:3 * cd + r]), qn_ref[...]).astype(BF16)
        ckv = _rms(_dot(xn, win_ref[:, 3 * cd + r:3 * cd + 2 * r]), kvn_ref[...]).astype(BF16)
        kr = _dot(xn, win_ref[:, 3 * cd + 2 * r:])
        lows.append((cq, ckv, kr))
    for rs, (cq, ckv, kr) in zip(chains, lows):
        c, s = c_ref[rs, :], s_ref[rs, :]
        q_ref[rs, :] = _rope(_dot(cq, wuq_ref[...]), c, s).astype(BF16)
        k_pre = _dot(ckv, wuk_ref[...]) + jnp.concatenate([kr] * MLA_HEADS, axis=1)
        k_ref[rs, :] = _rope(k_pre, c, s).astype(BF16)
        v_ref[rs, :] = _dot(ckv, wuv_ref[...]).astype(BF16)


def _hyb_in(x, g, win, convw, qn, kvn, wuq, wuk, wuv, layer, c_tab, s_tab, seq, tm, n_chains):
    t = x.shape[0]
    hp = MLA_HEADS * HEAD_PAD
    row = lambda n: pl.BlockSpec((tm, n), lambda i: (i, 0))
    full = _resident
    stacked = lambda a: _resident(a, layer)
    return pl.pallas_call(
        functools.partial(_hyb_in_kernel, tiles_per_seq=seq // tm, n_chains=n_chains),
        grid=(t // tm,),
        in_specs=[row(D_MODEL), full(g), stacked(win), full(convw), full(qn), full(kvn),
                  stacked(wuq), stacked(wuk), stacked(wuv), row(LANES), row(LANES)],
        out_specs=[row(CONV_DIM), row(hp), row(hp), row(MLA_HEADS * MLA_V)],
        out_shape=[jax.ShapeDtypeStruct((t, CONV_DIM), BF16),
                   jax.ShapeDtypeStruct((t, hp), BF16),
                   jax.ShapeDtypeStruct((t, hp), BF16),
                   jax.ShapeDtypeStruct((t, MLA_HEADS * MLA_V), BF16)],
        scratch_shapes=[pltpu.VMEM((tm + 8, CONV_DIM), F32)],
        compiler_params=_params("arbitrary"),
        name="hyb_in",
    )(x, g, win, convw, qn, kvn, wuq, wuk, wuv, c_tab, s_tab)


def _attn_kernel(q_ref, k_ref, v_ref, o_ref, *, blk, scale):
    seq = q_ref.shape[0]
    lane = lax.broadcasted_iota(jnp.int32, (blk, 2 * MLA_V), 1)
    first = lane < MLA_V
    rows = lax.broadcasted_iota(jnp.int32, (blk, blk), 0) // CHUNK
    cols = lax.broadcasted_iota(jnp.int32, (blk, blk), 1) // CHUNK
    visible = cols <= rows

    for i in range(seq // blk):
        q0 = i * blk
        outs = []
        for h in range(2):
            hs = slice(h * HEAD_PAD, (h + 1) * HEAD_PAD)
            qh = q_ref[q0:q0 + blk, hs]
            s_diag = jnp.where(visible, _dot_nt(qh, k_ref[q0:q0 + blk, hs]), -jnp.inf)
            m = jnp.max(s_diag, axis=1, keepdims=True)
            if i > 0:
                s_off = _dot_nt(qh, k_ref[0:q0, hs])
                m = jnp.maximum(m, jnp.max(s_off, axis=1, keepdims=True))
            p_diag = jnp.exp((s_diag - m) * scale)
            den = jnp.sum(p_diag, axis=1, keepdims=True)
            pv = _dot(p_diag.astype(BF16), v_ref[q0:q0 + blk, :])
            if i > 0:
                p_off = jnp.exp((s_off - m) * scale)
                den = den + jnp.sum(p_off, axis=1, keepdims=True)
                pv = pv + _dot(p_off.astype(BF16), v_ref[0:q0, :])
            outs.append(pv * (1.0 / den))
        o_ref[q0:q0 + blk, :] = jnp.where(first, outs[0], outs[1]).astype(o_ref.dtype)


def _attention(q, k, v, batch, seq, blk):
    t = q.shape[0]
    pairs = MLA_HEADS // 2
    scale = float((MLA_NOPE + MLA_ROPE) ** -0.5)
    spec = lambda n: pl.BlockSpec((seq, n), lambda b, p: (b, p))
    return pl.pallas_call(
        functools.partial(_attn_kernel, blk=blk, scale=scale),
        grid=(batch, pairs),
        in_specs=[spec(2 * HEAD_PAD), spec(2 * HEAD_PAD), spec(2 * MLA_V)],
        out_specs=spec(2 * MLA_V),
        out_shape=jax.ShapeDtypeStruct((t, MLA_HEADS * MLA_V), BF16),
        compiler_params=_params("parallel", "parallel"),
        name="mla_attention",
    )(q, k, v)


def _ml_in_kernel(x_ref, g_ref, win_ref, bias_ref, q_ref, k_ref, v_ref, o_ref, gate_ref, *,
                  n_chains):
    rc = x_ref.shape[0] // n_chains
    d = D_MODEL
    chains = [slice(c * rc, (c + 1) * rc) for c in range(n_chains)]
    xns = [_rms(x_ref[rs, :], g_ref[...]).astype(BF16) for rs in chains]
    for rs, xn in zip(chains, xns):
        q_ref[rs, :] = (_dot(xn, win_ref[:, 0:d]) * (ML_DH ** -0.5)).astype(BF16)
        k_ref[rs, :] = _dot(xn, win_ref[:, d:2 * d]).astype(BF16)
        v_ref[rs, :] = _dot(xn, win_ref[:, 2 * d:3 * d]).astype(BF16)
        o_ref[rs, :] = _dot(xn, win_ref[:, 3 * d:4 * d])
        gate_ref[rs, :] = _dot(xn, win_ref[:, 4 * d:]) + bias_ref[...]


def _ml_in(x, g, win, layer, bias, tm, n_chains):
    t = x.shape[0]
    row = lambda n: pl.BlockSpec((tm, n), lambda i: (i, 0))
    act = jax.ShapeDtypeStruct((t, D_MODEL), BF16)
    return pl.pallas_call(
        functools.partial(_ml_in_kernel, n_chains=n_chains),
        grid=(t // tm,),
        in_specs=[row(D_MODEL), _resident(g), _resident(win, layer), _resident(bias)],
        out_specs=[row(D_MODEL)] * 4 + [row(LANES)],
        out_shape=[act, act, act, jax.ShapeDtypeStruct((t, D_MODEL), F32),
                   jax.ShapeDtypeStruct((t, LANES), F32)],
        compiler_params=_params("parallel"),
        name="mlstm_in",
    )(x, g, win, bias)


def _log_sigmoid(x):
    return jnp.minimum(x, 0.0) - jnp.log1p(jnp.exp(-jnp.abs(x)))


def _ml_cell_kernel(q_ref, k_ref, v_ref, o_ref, gate_ref, hn_ref, out_ref, c_sc, n_sc, m_sc):
    ln = q_ref.shape[0]

    @pl.when(pl.program_id(1) == 0)
    def _():
        c_sc[...] = jnp.zeros(c_sc.shape, F32)
        n_sc[...] = jnp.zeros(n_sc.shape, F32)
        m_sc[...] = jnp.zeros(m_sc.shape, F32)

    heads = range(ML_HEADS)
    sls = [slice(h * ML_DH, (h + 1) * ML_DH) for h in heads]
    q = [q_ref[:, sl] for sl in sls]
    k = [k_ref[:, sl] for sl in sls]
    v = [v_ref[:, sl] for sl in sls]

    c_prev = [c_sc[h] for h in heads]
    qk = [_dot_nt(q[h], k[h]) for h in heads]
    qc = [_dot(q[h], c_prev[h].astype(BF16)) for h in heads]

    gates = gate_ref[...]
    lsig = _log_sigmoid(gates)
    t_idx = lax.broadcasted_iota(jnp.int32, (ln, ln), 0)
    s_idx = lax.broadcasted_iota(jnp.int32, (ln, ln), 1)
    causal = s_idx <= t_idx
    tril = jnp.where(causal, 1.0, 0.0).astype(BF16)
    hi = lsig.astype(BF16)
    rest = lsig - hi.astype(F32)
    mid = rest.astype(BF16)
    low = (rest - mid.astype(F32)).astype(BF16)
    bcum = _dot(tril, hi) + _dot(tril, mid) + _dot(tril, low)
    gates_t = gates.T
    bcum_t = bcum.T

    scores, kws, inters, m_ts, dens, stats = [], [], [], [], [], []
    for h in heads:
        li_col = gates[:, h:h + 1]
        li_row = gates_t[h:h + 1, :]
        b_col = bcum[:, ML_HEADS + h:ML_HEADS + h + 1]
        b_row = bcum_t[ML_HEADS + h:ML_HEADS + h + 1, :]
        d_log = jnp.where(causal, b_col - b_row + li_row, -jnp.inf)
        m_intra = jnp.max(d_log, axis=1, keepdims=True)
        m_prev = m_sc[h][:, 0:1]
        m_t = jnp.maximum(b_col + m_prev, m_intra)
        inter = jnp.exp(b_col + m_prev - m_t)
        sc = qk[h] * jnp.exp(d_log - m_t)
        n_prev = n_sc[h]
        dens.append(inter * jnp.sum(q[h].astype(F32) * n_prev, axis=1, keepdims=True)
                    + jnp.sum(sc, axis=1, keepdims=True))
        scores.append(sc.astype(BF16))
        inters.append(inter)
        m_ts.append(m_t)
        b_last = b_col[ln - 1:ln, :]
        a_col = b_last - b_col + li_col
        a_max = jnp.max(a_col, axis=0, keepdims=True)
        kw = k[h].astype(F32) * jnp.exp(a_col - a_max)
        kws.append(kw.astype(BF16))
        m_new = jnp.maximum(b_last + m_prev, a_max)
        sp = jnp.exp(b_last + m_prev - m_new)
        sn = jnp.exp(a_max - m_new)
        n_sc[h] = sp * n_prev + sn * jnp.sum(kw, axis=0, keepdims=True)
        m_sc[h] = jnp.broadcast_to(m_new, (1, LANES))
        stats.append((sp, sn))

    sv = [_dot(scores[h], v[h]) for h in heads]
    c_loc = [_dot_tn(kws[h], v[h]) for h in heads]

    for h in heads:
        num = inters[h] * qc[h] + sv[h]
        cell = num * (1.0 / jnp.maximum(jnp.abs(dens[h]), jnp.exp(-m_ts[h])))
        normed = _rms(cell, hn_ref[:, sls[h]])
        out_ref[:, sls[h]] = (jax.nn.sigmoid(o_ref[:, sls[h]]) * normed).astype(out_ref.dtype)
        sp, sn = stats[h]
        c_sc[h] = sp * c_prev[h] + sn * c_loc[h]


def _ml_cell(q, k, v, o, gates, head_norm, batch, seq, ln):
    t = q.shape[0]
    nc = seq // ln
    row = lambda n: pl.BlockSpec((ln, n), lambda b, c: (b * nc + c, 0))
    return pl.pallas_call(
        _ml_cell_kernel,
        grid=(batch, nc),
        in_specs=[row(D_MODEL)] * 4 + [row(LANES), pl.BlockSpec((1, D_MODEL), lambda b, c: (0, 0))],
        out_specs=row(D_MODEL),
        out_shape=jax.ShapeDtypeStruct((t, D_MODEL), BF16),
        scratch_shapes=[pltpu.VMEM((ML_HEADS, ML_DH, ML_DH), F32),
                        pltpu.VMEM((ML_HEADS, 1, ML_DH), F32),
                        pltpu.VMEM((ML_HEADS, 1, LANES), F32)],
        compiler_params=_params("parallel", "arbitrary"),
        name="mlstm_cell",
    )(q, k, v, o, gates, head_norm)


def _out_mlp_kernel(*refs, n_parts, n_chains):
    parts = refs[:n_parts]
    wout_ref, x_ref, gpost_ref, gpre_ref, gmlp_ref, w1_ref, w2_ref, out_ref = refs[n_parts:]
    rc = x_ref.shape[0] // n_chains
    x1s = []
    for c in range(n_chains):
        rs = slice(c * rc, (c + 1) * rc)
        y, r0 = None, 0
        for p in parts:
            kp = p.shape[1]
            d = _dot(p[rs, :], wout_ref[r0:r0 + kp, :])
            y = d if y is None else y + d
            r0 += kp
        x1s.append(x_ref[rs, :] + _rms(y, gpost_ref[...]))
    for c in range(n_chains):
        rs = slice(c * rc, (c + 1) * rc)
        hn = _rms(x1s[c], gpre_ref[...]).astype(BF16)
        a = jnp.square(jnp.maximum(_dot(hn, w1_ref[...]), 0.0)).astype(BF16)
        out_ref[rs, :] = x1s[c] + _rms(_dot(a, w2_ref[...]), gmlp_ref[...])


def _out_mlp(parts, wout, wout_layer, x, gpost, gpre, gmlp, w1, w2, mlp_layer, tm, n_chains):
    t = x.shape[0]
    row = lambda n: pl.BlockSpec((tm, n), lambda i: (i, 0))
    vec = _resident(gpost)
    return pl.pallas_call(
        functools.partial(_out_mlp_kernel, n_parts=len(parts), n_chains=n_chains),
        grid=(t // tm,),
        in_specs=[row(p.shape[1]) for p in parts]
        + [_resident(wout, wout_layer), row(D_MODEL), vec, vec, vec,
           _resident(w1, mlp_layer), _resident(w2, mlp_layer)],
        out_specs=row(D_MODEL),
        out_shape=jax.ShapeDtypeStruct((t, D_MODEL), F32),
        compiler_params=_params("parallel"),
        name="out_mlp",
    )(*parts, wout, x, gpost, gpre, gmlp, w1, w2)


def _pad_heads(w, width):
    n, r = w.shape[:2]
    w = jnp.pad(w.astype(BF16), ((0, 0), (0, 0), (0, 0), (0, HEAD_PAD - width)))
    return w.reshape(n, r, MLA_HEADS * HEAD_PAD)


def _hyb_weights(w_in, w_uq, w_ukv):
    n = w_in.shape[0]
    split = 3 * CONV_DIM + 2 * MLA_RANK
    k_r = jnp.pad(w_in[:, :, split:], ((0, 0), (0, 0), (MLA_NOPE, LANES - MLA_NOPE - MLA_ROPE)))
    win = jnp.concatenate([w_in[:, :, :split], k_r], axis=2).astype(BF16)
    wuq = _pad_heads(w_uq.reshape(n, MLA_RANK, MLA_HEADS, MLA_NOPE + MLA_ROPE), MLA_NOPE + MLA_ROPE)
    ukv = w_ukv.reshape(n, MLA_RANK, MLA_HEADS, MLA_NOPE + MLA_V)
    wuk = _pad_heads(ukv[..., :MLA_NOPE], MLA_NOPE)
    wuv = ukv[..., MLA_NOPE:].reshape(n, MLA_RANK, MLA_HEADS * MLA_V).astype(BF16)
    return win, wuq, wuk, wuv


def kernel(x, positions, norm_mix_pre, norm_mix_post, norm_mlp_pre, norm_mlp_post, hyb_w_in, conv_w, mla_q_norm, mla_kv_norm, mla_w_uq, mla_w_ukv, hyb_w_out, ml_w_in, ml_b_i, ml_b_f, ml_head_norm, ml_w_out, mlp_w1, mlp_w2):
    batch, seq, d = x.shape
    t = batch * seq
    depth = norm_mix_pre.shape[0]
    tm_in = min(1024, seq)
    in_chains = 4
    tm_mlp = min(1024, t)
    mlp_chains = 4
    blk = min(256, seq)

    xt = x.reshape(t, d)
    c_tab, s_tab = _rope_tables(positions)
    vec = lambda a: a.reshape(1, -1)

    hyb_win, hyb_wuq, hyb_wuk, hyb_wuv = _hyb_weights(hyb_w_in, mla_w_uq, mla_w_ukv)
    hyb_wout = hyb_w_out.astype(BF16)
    ml_win = jnp.pad(ml_w_in.astype(BF16), ((0, 0), (0, 0), (0, ML_N - ml_w_in.shape[2])))
    ml_wout = ml_w_out.astype(BF16)
    w1 = mlp_w1.astype(BF16)
    w2 = mlp_w2.astype(BF16)

    for l in range(depth):
        e = l // 2
        if l % 2 == 0:
            y_a, q, k, v = _hyb_in(xt, vec(norm_mix_pre[l]), hyb_win, conv_w[e], vec(mla_q_norm[e]),
                                   vec(mla_kv_norm[e]), hyb_wuq, hyb_wuk, hyb_wuv, e, c_tab, s_tab,
                                   seq, tm_in, in_chains)
            y_b = _attention(q, k, v, batch, seq, blk)
            parts, wout = [y_a, y_b], hyb_wout
        else:
            bias = jnp.pad(jnp.concatenate([ml_b_i[e], ml_b_f[e]]), (0, LANES - 2 * ML_HEADS))[None, :]
            q, k, v, o, gates = _ml_in(xt, vec(norm_mix_pre[l]), ml_win, e, bias, tm_in, in_chains)
            cell = _ml_cell(q, k, v, o, gates, vec(ml_head_norm[e]), batch, seq, blk)
            parts, wout = [cell], ml_wout
        xt = _out_mlp(parts, wout, e, xt, vec(norm_mix_post[l]), vec(norm_mlp_pre[l]),
                      vec(norm_mlp_post[l]), w1, w2, l, tm_mlp, mlp_chains)
    return xt.reshape(batch, seq, d)
```

```python
import functools

import jax
import jax.numpy as jnp
import numpy as np
from jax import lax
from jax.experimental import pallas as pl
from jax.experimental.pallas import tpu as pltpu

D_MODEL = 1024
EPS = 1e-6
CHUNK = 64
CONV_DIM = 512
CONV_WIDTH = 3
MLA_HEADS = 8
MLA_RANK = 256
MLA_NOPE = 64
MLA_ROPE = 32
MLA_V = 64
ROPE_THETA = 10000.0
ML_HEADS = 4
ML_DH = 256
D_FF = 4096

ATTN_SCALE = (MLA_NOPE + MLA_ROPE) ** -0.5
LOG2_E = 1.4426950408889634

LANES = 128
HEAD_PAD = LANES
HYB_N = 4 * CONV_DIM + LANES
ML_N = 4 * D_MODEL + LANES
VMEM_LIMIT = 56 * 1024 * 1024

BF16 = jnp.bfloat16
F32 = jnp.float32


def _dot(a, b):
    return jnp.dot(a, b, preferred_element_type=F32)


def _dot_nt(a, b):
    return lax.dot_general(a, b, (((1,), (1,)), ((), ())), preferred_element_type=F32)


def _dot_tn(a, b):
    return lax.dot_general(a, b, (((0,), (0,)), ((), ())), preferred_element_type=F32)


def _rms(x, g):
    return x * lax.rsqrt(jnp.mean(x * x, axis=-1, keepdims=True) + EPS) * g


def _params(*semantics):
    return pltpu.CompilerParams(dimension_semantics=semantics, vmem_limit_bytes=VMEM_LIMIT)


def _resident(a, layer=None):
    if layer is None:
        return pl.BlockSpec(a.shape, lambda *_: (0,) * a.ndim, pipeline_mode=pl.Buffered(1))
    return pl.BlockSpec((None,) + a.shape[1:], lambda *_: (layer,) + (0,) * (a.ndim - 1),
                        pipeline_mode=pl.Buffered(1))


def _rope_table_kernel(pos_ref, inv_ref, sign_ref, c_ref, s_ref):
    ang = pos_ref[...].astype(F32) * inv_ref[...]
    c_ref[...] = jnp.cos(ang)
    s_ref[...] = sign_ref[...] * jnp.sin(ang)


def _head_lanes():
    half = MLA_ROPE // 2
    x1 = LANES // 2 - half
    src = -np.ones((LANES,), np.int32)
    src[:x1] = np.arange(x1)
    src[x1:x1 + half] = MLA_NOPE + np.arange(half)
    src[LANES // 2:LANES // 2 + MLA_NOPE - x1] = x1 + np.arange(MLA_NOPE - x1)
    src[LANES - half:] = MLA_NOPE + half + np.arange(half)
    return src


def _place(w, src):
    n = w.shape[-1]
    idx = np.where((src >= 0) & (src < n), src, n)
    return jnp.take(jnp.pad(w, [(0, 0)] * (w.ndim - 1) + [(0, 1)]), idx, axis=-1)


def _rope_tables(positions):
    t = positions.size
    rows = min(t, 2048)
    half = MLA_ROPE // 2
    inv = ROPE_THETA ** (-jnp.arange(half, dtype=F32) / half)
    rope_src = _head_lanes() - MLA_NOPE
    inv_lane = _place(jnp.concatenate([inv, inv]), np.where(rope_src >= 0, rope_src, -1))[None, :]
    sign_lane = _place(jnp.concatenate([-jnp.ones((half,), F32), jnp.ones((half,), F32)]),
                       np.where(rope_src >= 0, rope_src, -1))[None, :]
    row = pl.BlockSpec((rows, LANES), lambda i: (i, 0))
    const = pl.BlockSpec((1, LANES), lambda i: (0, 0))
    return pl.pallas_call(
        _rope_table_kernel,
        grid=(t // rows,),
        in_specs=[pl.BlockSpec((rows, 1), lambda i: (i, 0)), const, const],
        out_specs=[row, row],
        out_shape=[jax.ShapeDtypeStruct((t, LANES), F32)] * 2,
        compiler_params=_params("parallel"),
        name="rope_tables",
    )(positions.reshape(t, 1), inv_lane, sign_lane)


def _rope(v, c, s):
    out = []
    for h in range(v.shape[1] // HEAD_PAD):
        vb = v[:, h * HEAD_PAD:(h + 1) * HEAD_PAD]
        out.append(vb * c + pltpu.roll(vb, HEAD_PAD // 2, 1) * s)
    return out[0] if len(out) == 1 else jnp.concatenate(out, axis=1)


def _hyb_in_kernel(x_ref, g_ref, win_ref, convw_ref, qn_ref, kvn_ref, wuq_ref, wuk_ref,
                   wuv_ref, c_ref, s_ref, ya_ref, q_ref, k_ref, v_ref, u_sc, *, tiles_per_seq,
                   n_chains, q_scale):
    tm = x_ref.shape[0]
    rc = tm // n_chains
    cd, r = CONV_DIM, MLA_RANK

    @pl.when(pl.program_id(0) % tiles_per_seq == 0)
    def _():
        u_sc[0:8, :] = jnp.zeros((8, cd), F32)

    @pl.when(pl.program_id(0) % tiles_per_seq != 0)
    def _():
        u_sc[0:8, :] = u_sc[tm:tm + 8, :]

    chains = [slice(c * rc, (c + 1) * rc) for c in range(n_chains)]
    xns = [_rms(x_ref[rs, :], g_ref[...]).astype(BF16) for rs in chains]
    w = convw_ref[...]
    def down(rs, xn):
        b_gate = _dot(xn, win_ref[:, 0:cd])
        u = _dot(xn, win_ref[:, cd:2 * cd]) * _dot(xn, win_ref[:, 2 * cd:3 * cd])
        u_sc[8 + rs.start:8 + rs.stop, :] = u
        y = (w[2:3, :] * u + w[1:2, :] * u_sc[7 + rs.start:7 + rs.stop, :]
             + w[0:1, :] * u_sc[6 + rs.start:6 + rs.stop, :])
        ya_ref[rs, :] = (b_gate * y).astype(BF16)
        cq = _rms(_dot(xn, win_ref[:, 3 * cd:3 * cd + r]), qn_ref[...]).astype(BF16)
        ckv = _rms(_dot(xn, win_ref[:, 3 * cd + r:3 * cd + 2 * r]), kvn_ref[...]).astype(BF16)
        kr = _dot(xn, win_ref[:, 3 * cd + 2 * r:])
        return cq, ckv, kr

    def up(rs, cq, ckv, kr):
        c, s = c_ref[rs, :], s_ref[rs, :]
        q_ref[rs, :] = (_rope(_dot(cq, wuq_ref[...]), c, s) * q_scale).astype(BF16)
        k_rope = _rope(kr, c, s)
        k_ref[rs, :] = (_dot(ckv, wuk_ref[...])
                        + jnp.concatenate([k_rope] * MLA_HEADS, axis=1)).astype(BF16)
        v_ref[rs, :] = _dot(ckv, wuv_ref[...]).astype(BF16)

    low = None
    for c, (rs, xn) in enumerate(zip(chains, xns)):
        nxt = down(rs, xn)
        if low is not None:
            up(chains[c - 1], *low)
        low = nxt
    up(chains[-1], *low)


def _hyb_in(x, g, win, convw, qn, kvn, wuq, wuk, wuv, layer, c_tab, s_tab, seq, tm, n_chains):
    t = x.shape[0]
    hp = MLA_HEADS * HEAD_PAD
    row = lambda n: pl.BlockSpec((tm, n), lambda i: (i, 0))
    full = _resident
    stacked = lambda a: _resident(a, layer)
    return pl.pallas_call(
        functools.partial(_hyb_in_kernel, tiles_per_seq=seq // tm, n_chains=n_chains,
                          q_scale=ATTN_SCALE * LOG2_E),
        grid=(t // tm,),
        in_specs=[row(D_MODEL), full(g), stacked(win), full(convw), full(qn), full(kvn),
                  stacked(wuq), stacked(wuk), stacked(wuv), row(LANES), row(LANES)],
        out_specs=[row(CONV_DIM), row(hp), row(hp), row(MLA_HEADS * MLA_V)],
        out_shape=[jax.ShapeDtypeStruct((t, CONV_DIM), BF16),
                   jax.ShapeDtypeStruct((t, hp), BF16),
                   jax.ShapeDtypeStruct((t, hp), BF16),
                   jax.ShapeDtypeStruct((t, MLA_HEADS * MLA_V), BF16)],
        scratch_shapes=[pltpu.VMEM((tm + 8, CONV_DIM), F32)],
        compiler_params=_params("arbitrary"),
        name="hyb_in",
    )(x, g, win, convw, qn, kvn, wuq, wuk, wuv, c_tab, s_tab)


def _attn_kernel(q_ref, k_ref, v_ref, o_ref, *, blk):
    seq = q_ref.shape[0]
    lane = lax.broadcasted_iota(jnp.int32, (blk, 2 * MLA_V), 1)
    first = lane < MLA_V
    rows = lax.broadcasted_iota(jnp.int32, (blk, blk), 0) // CHUNK
    cols = lax.broadcasted_iota(jnp.int32, (blk, blk), 1) // CHUNK
    visible = cols <= rows

    def scores(i, h):
        q0 = i * blk
        hs = slice(h * HEAD_PAD, (h + 1) * HEAD_PAD)
        qh = q_ref[q0:q0 + blk, hs]
        s_diag = jnp.where(visible, _dot_nt(qh, k_ref[q0:q0 + blk, hs]), -jnp.inf)
        s_off = _dot_nt(qh, k_ref[0:q0, hs]) if i > 0 else None
        return s_diag, s_off

    def attend(i, s_diag, s_off):
        q0 = i * blk
        m = jnp.max(s_diag, axis=1, keepdims=True)
        if s_off is not None:
            m = jnp.maximum(m, jnp.max(s_off, axis=1, keepdims=True))
        p_diag = jnp.exp2(s_diag - m)
        den = jnp.sum(p_diag, axis=1, keepdims=True)
        pv = _dot(p_diag.astype(BF16), v_ref[q0:q0 + blk, :])
        if s_off is not None:
            p_off = jnp.exp2(s_off - m)
            den = den + jnp.sum(p_off, axis=1, keepdims=True)
            pv = pv + _dot(p_off.astype(BF16), v_ref[0:q0, :])
        return pv * (1.0 / den)

    units = [(i, h) for i in range(seq // blk) for h in range(2)]
    pending = scores(*units[0])
    outs = []
    for u, (i, h) in enumerate(units):
        ahead = scores(*units[u + 1]) if u + 1 < len(units) else None
        outs.append(attend(i, *pending))
        pending = ahead
        if h == 1:
            o_ref[i * blk:(i + 1) * blk, :] = jnp.where(first, outs[0], outs[1]).astype(o_ref.dtype)
            outs = []


def _attention(q, k, v, batch, seq, blk):
    t = q.shape[0]
    pairs = MLA_HEADS // 2
    spec = lambda n: pl.BlockSpec((seq, n), lambda b, p: (b, p))
    return pl.pallas_call(
        functools.partial(_attn_kernel, blk=blk),
        grid=(batch, pairs),
        in_specs=[spec(2 * HEAD_PAD), spec(2 * HEAD_PAD), spec(2 * MLA_V)],
        out_specs=spec(2 * MLA_V),
        out_shape=jax.ShapeDtypeStruct((t, MLA_HEADS * MLA_V), BF16),
        compiler_params=_params("parallel", "parallel"),
        name="mla_attention",
    )(q, k, v)


def _ml_in_kernel(x_ref, g_ref, win_ref, bias_ref, q_ref, k_ref, v_ref, o_ref, gate_ref, *,
                  n_chains):
    rc = x_ref.shape[0] // n_chains
    d = D_MODEL
    chains = [slice(c * rc, (c + 1) * rc) for c in range(n_chains)]
    xns = [_rms(x_ref[rs, :], g_ref[...]).astype(BF16) for rs in chains]
    for rs, xn in zip(chains, xns):
        q_ref[rs, :] = (_dot(xn, win_ref[:, 0:d]) * (ML_DH ** -0.5)).astype(BF16)
        k_ref[rs, :] = _dot(xn, win_ref[:, d:2 * d]).astype(BF16)
        v_ref[rs, :] = _dot(xn, win_ref[:, 2 * d:3 * d]).astype(BF16)
        o_ref[rs, :] = _dot(xn, win_ref[:, 3 * d:4 * d])
        gate_ref[rs, :] = _dot(xn, win_ref[:, 4 * d:]) + bias_ref[...]


def _ml_in(x, g, win, layer, bias, tm, n_chains):
    t = x.shape[0]
    row = lambda n: pl.BlockSpec((tm, n), lambda i: (i, 0))
    act = jax.ShapeDtypeStruct((t, D_MODEL), BF16)
    return pl.pallas_call(
        functools.partial(_ml_in_kernel, n_chains=n_chains),
        grid=(t // tm,),
        in_specs=[row(D_MODEL), _resident(g), _resident(win, layer), _resident(bias)],
        out_specs=[row(D_MODEL)] * 4 + [row(LANES)],
        out_shape=[act, act, act, jax.ShapeDtypeStruct((t, D_MODEL), F32),
                   jax.ShapeDtypeStruct((t, LANES), F32)],
        compiler_params=_params("parallel"),
        name="mlstm_in",
    )(x, g, win, bias)


def _log_sigmoid(x):
    return jnp.minimum(x, 0.0) - jnp.log1p(jnp.exp(-jnp.abs(x)))


def _ml_cell_kernel(q_ref, k_ref, v_ref, o_ref, gate_ref, hn_ref, out_ref, c_sc, n_sc, m_sc):
    ln = q_ref.shape[0]

    @pl.when(pl.program_id(1) == 0)
    def _():
        c_sc[...] = jnp.zeros(c_sc.shape, F32)
        n_sc[...] = jnp.zeros(n_sc.shape, F32)
        m_sc[...] = jnp.zeros(m_sc.shape, F32)

    heads = range(ML_HEADS)
    sls = [slice(h * ML_DH, (h + 1) * ML_DH) for h in heads]
    q = [q_ref[:, sl] for sl in sls]
    k = [k_ref[:, sl] for sl in sls]
    v = [v_ref[:, sl] for sl in sls]

    c_prev = [c_sc[h] for h in heads]
    qk = [_dot_nt(q[h], k[h]) for h in heads]
    qc = [_dot(q[h], c_prev[h].astype(BF16)) for h in heads]

    gates = gate_ref[...]
    lsig = _log_sigmoid(gates)
    t_idx = lax.broadcasted_iota(jnp.int32, (ln, ln), 0)
    s_idx = lax.broadcasted_iota(jnp.int32, (ln, ln), 1)
    causal = s_idx <= t_idx
    tril = jnp.where(causal, 1.0, 0.0).astype(BF16)
    hi = lsig.astype(BF16)
    rest = lsig - hi.astype(F32)
    mid = rest.astype(BF16)
    low = (rest - mid.astype(F32)).astype(BF16)
    bcum = _dot(tril, hi) + _dot(tril, mid) + _dot(tril, low)
    gates_t = gates.T
    bcum_t = bcum.T

    scores, kws, inters, m_ts, dens, stats = [], [], [], [], [], []
    for h in heads:
        li_col = gates[:, h:h + 1]
        li_row = gates_t[h:h + 1, :]
        b_col = bcum[:, ML_HEADS + h:ML_HEADS + h + 1]
        b_row = bcum_t[ML_HEADS + h:ML_HEADS + h + 1, :]
        d_log = jnp.where(causal, b_col - b_row + li_row, -jnp.inf)
        m_intra = jnp.max(d_log, axis=1, keepdims=True)
        m_prev = m_sc[h][:, 0:1]
        m_t = jnp.maximum(b_col + m_prev, m_intra)
        inter = jnp.exp(b_col + m_prev - m_t)
        sc = qk[h] * jnp.exp(d_log - m_t)
        n_prev = n_sc[h]
        dens.append(inter * jnp.sum(q[h].astype(F32) * n_prev, axis=1, keepdims=True)
                    + jnp.sum(sc, axis=1, keepdims=True))
        scores.append(sc.astype(BF16))
        inters.append(inter)
        m_ts.append(m_t)
        b_last = b_col[ln - 1:ln, :]
        a_col = b_last - b_col + li_col
        a_max = jnp.max(a_col, axis=0, keepdims=True)
        kw = k[h].astype(F32) * jnp.exp(a_col - a_max)
        kws.append(kw.astype(BF16))
        m_new = jnp.maximum(b_last + m_prev, a_max)
        sp = jnp.exp(b_last + m_prev - m_new)
        sn = jnp.exp(a_max - m_new)
        n_sc[h] = sp * n_prev + sn * jnp.sum(kw, axis=0, keepdims=True)
        m_sc[h] = jnp.broadcast_to(m_new, (1, LANES))
        stats.append((sp, sn))

    sv = [_dot(scores[h], v[h]) for h in heads]
    c_loc = [_dot_tn(kws[h], v[h]) for h in heads]

    for h in heads:
        num = inters[h] * qc[h] + sv[h]
        cell = num * (1.0 / jnp.maximum(jnp.abs(dens[h]), jnp.exp(-m_ts[h])))
        normed = _rms(cell, hn_ref[:, sls[h]])
        out_ref[:, sls[h]] = (jax.nn.sigmoid(o_ref[:, sls[h]]) * normed).astype(out_ref.dtype)
        sp, sn = stats[h]
        c_sc[h] = sp * c_prev[h] + sn * c_loc[h]


def _ml_cell(q, k, v, o, gates, head_norm, batch, seq, ln):
    t = q.shape[0]
    nc = seq // ln
    row = lambda n: pl.BlockSpec((ln, n), lambda b, c: (b * nc + c, 0))
    return pl.pallas_call(
        _ml_cell_kernel,
        grid=(batch, nc),
        in_specs=[row(D_MODEL)] * 4 + [row(LANES), pl.BlockSpec((1, D_MODEL), lambda b, c: (0, 0))],
        out_specs=row(D_MODEL),
        out_shape=jax.ShapeDtypeStruct((t, D_MODEL), BF16),
        scratch_shapes=[pltpu.VMEM((ML_HEADS, ML_DH, ML_DH), F32),
                        pltpu.VMEM((ML_HEADS, 1, ML_DH), F32),
                        pltpu.VMEM((ML_HEADS, 1, LANES), F32)],
        compiler_params=_params("parallel", "arbitrary"),
        name="mlstm_cell",
    )(q, k, v, o, gates, head_norm)


def _out_mlp_kernel(*refs, n_parts, n_chains):
    parts = refs[:n_parts]
    wout_ref, x_ref, gpost_ref, gpre_ref, gmlp_ref, w1_ref, w2_ref, out_ref = refs[n_parts:]
    rc = x_ref.shape[0] // n_chains
    x1s = []
    for c in range(n_chains):
        rs = slice(c * rc, (c + 1) * rc)
        y, r0 = None, 0
        for p in parts:
            kp = p.shape[1]
            d = _dot(p[rs, :], wout_ref[r0:r0 + kp, :])
            y = d if y is None else y + d
            r0 += kp
        x1s.append(x_ref[rs, :] + _rms(y, gpost_ref[...]))
    for c in range(n_chains):
        rs = slice(c * rc, (c + 1) * rc)
        hn = _rms(x1s[c], gpre_ref[...]).astype(BF16)
        a = jnp.square(jnp.maximum(_dot(hn, w1_ref[...]), 0.0)).astype(BF16)
        out_ref[rs, :] = x1s[c] + _rms(_dot(a, w2_ref[...]), gmlp_ref[...])


def _out_mlp(parts, wout, wout_layer, x, gpost, gpre, gmlp, w1, w2, mlp_layer, tm, n_chains):
    t = x.shape[0]
    row = lambda n: pl.BlockSpec((tm, n), lambda i: (i, 0))
    vec = _resident(gpost)
    return pl.pallas_call(
        functools.partial(_out_mlp_kernel, n_parts=len(parts), n_chains=n_chains),
        grid=(t // tm,),
        in_specs=[row(p.shape[1]) for p in parts]
        + [_resident(wout, wout_layer), row(D_MODEL), vec, vec, vec,
           _resident(w1, mlp_layer), _resident(w2, mlp_layer)],
        out_specs=row(D_MODEL),
        out_shape=jax.ShapeDtypeStruct((t, D_MODEL), F32),
        compiler_params=_params("parallel"),
        name="out_mlp",
    )(*parts, wout, x, gpost, gpre, gmlp, w1, w2)


def _hyb_weights(w_in, w_uq, w_ukv):
    n = w_in.shape[0]
    split = 3 * CONV_DIM + 2 * MLA_RANK
    src = _head_lanes()
    heads = lambda w: w.astype(BF16).reshape(n, MLA_RANK, MLA_HEADS * HEAD_PAD)
    k_r = _place(w_in[:, :, split:], np.where(src >= MLA_NOPE, src - MLA_NOPE, -1))
    win = jnp.concatenate([w_in[:, :, :split], k_r], axis=2).astype(BF16)
    wuq = heads(_place(w_uq.reshape(n, MLA_RANK, MLA_HEADS, MLA_NOPE + MLA_ROPE), src))
    ukv = w_ukv.reshape(n, MLA_RANK, MLA_HEADS, MLA_NOPE + MLA_V)
    wuk = heads(_place(ukv[..., :MLA_NOPE], src))
    wuv = ukv[..., MLA_NOPE:].reshape(n, MLA_RANK, MLA_HEADS * MLA_V).astype(BF16)
    return win, wuq, wuk, wuv


def kernel(x, positions, norm_mix_pre, norm_mix_post, norm_mlp_pre, norm_mlp_post, hyb_w_in, conv_w, mla_q_norm, mla_kv_norm, mla_w_uq, mla_w_ukv, hyb_w_out, ml_w_in, ml_b_i, ml_b_f, ml_head_norm, ml_w_out, mlp_w1, mlp_w2):
    batch, seq, d = x.shape
    t = batch * seq
    depth = norm_mix_pre.shape[0]
    tm_in = min(1024, seq)
    in_chains = 4
    tm_mlp = min(1024, t)
    mlp_chains = 4
    blk = min(256, seq)

    xt = x.reshape(t, d)
    c_tab, s_tab = _rope_tables(positions)
    vec = lambda a: a.reshape(1, -1)

    hyb_win, hyb_wuq, hyb_wuk, hyb_wuv = _hyb_weights(hyb_w_in, mla_w_uq, mla_w_ukv)
    hyb_wout = hyb_w_out.astype(BF16)
    ml_win = jnp.pad(ml_w_in.astype(BF16), ((0, 0), (0, 0), (0, ML_N - ml_w_in.shape[2])))
    ml_wout = ml_w_out.astype(BF16)
    w1 = mlp_w1.astype(BF16)
    w2 = mlp_w2.astype(BF16)

    for l in range(depth):
        e = l // 2
        if l % 2 == 0:
            y_a, q, k, v = _hyb_in(xt, vec(norm_mix_pre[l]), hyb_win, conv_w[e], vec(mla_q_norm[e]),
                                   vec(mla_kv_norm[e]), hyb_wuq, hyb_wuk, hyb_wuv, e, c_tab, s_tab,
                                   seq, tm_in, in_chains)
            y_b = _attention(q, k, v, batch, seq, blk)
            parts, wout = [y_a, y_b], hyb_wout
        else:
            bias = jnp.pad(jnp.concatenate([ml_b_i[e], ml_b_f[e]]), (0, LANES - 2 * ML_HEADS))[None, :]
            q, k, v, o, gates = _ml_in(xt, vec(norm_mix_pre[l]), ml_win, e, bias, tm_in, in_chains)
            cell = _ml_cell(q, k, v, o, gates, vec(ml_head_norm[e]), batch, seq, blk)
            parts, wout = [cell], ml_wout
        xt = _out_mlp(parts, wout, e, xt, vec(norm_mix_post[l]), vec(norm_mlp_pre[l]),
                      vec(norm_mlp_post[l]), w1, w2, l, tm_mlp, mlp_chains)
    return xt.reshape(batch, seq, d)
```

```python
import functools

import jax
import jax.numpy as jnp
import numpy as np
from jax import lax
from jax.experimental import pallas as pl
from jax.experimental.pallas import tpu as pltpu

D_MODEL = 1024
EPS = 1e-6
CHUNK = 64
CONV_DIM = 512
CONV_WIDTH = 3
MLA_HEADS = 8
MLA_RANK = 256
MLA_NOPE = 64
MLA_ROPE = 32
MLA_V = 64
ROPE_THETA = 10000.0
ML_HEADS = 4
ML_DH = 256
D_FF = 4096

ATTN_SCALE = (MLA_NOPE + MLA_ROPE) ** -0.5
LOG2_E = 1.4426950408889634

LANES = 128
HEAD_PAD = LANES
ML_EXT = 16
VMEM_LIMIT = 56 * 1024 * 1024

BF16 = jnp.bfloat16
F32 = jnp.float32


def _dot(a, b):
    return jnp.dot(a, b, preferred_element_type=F32)


def _dot_nt(a, b):
    return lax.dot_general(a, b, (((1,), (1,)), ((), ())), preferred_element_type=F32)


def _rms(x, g):
    return x * lax.rsqrt(jnp.mean(x * x, axis=-1, keepdims=True) + EPS) * g


def _params(*semantics):
    return pltpu.CompilerParams(dimension_semantics=semantics, vmem_limit_bytes=VMEM_LIMIT)


def _resident(a, layer=None):
    if layer is None:
        return pl.BlockSpec(a.shape, lambda *_: (0,) * a.ndim, pipeline_mode=pl.Buffered(1))
    return pl.BlockSpec((None,) + a.shape[1:], lambda *_: (layer,) + (0,) * (a.ndim - 1),
                        pipeline_mode=pl.Buffered(1))


def _rope_table_kernel(pos_ref, inv_ref, sign_ref, c_ref, s_ref):
    ang = pos_ref[...].astype(F32) * inv_ref[...]
    c_ref[...] = jnp.cos(ang)
    s_ref[...] = sign_ref[...] * jnp.sin(ang)


def _head_lanes():
    half = MLA_ROPE // 2
    x1 = LANES // 2 - half
    src = -np.ones((LANES,), np.int32)
    src[:x1] = np.arange(x1)
    src[x1:x1 + half] = MLA_NOPE + np.arange(half)
    src[LANES // 2:LANES // 2 + MLA_NOPE - x1] = x1 + np.arange(MLA_NOPE - x1)
    src[LANES - half:] = MLA_NOPE + half + np.arange(half)
    return src


def _place(w, src):
    n = w.shape[-1]
    idx = np.where((src >= 0) & (src < n), src, n)
    return jnp.take(jnp.pad(w, [(0, 0)] * (w.ndim - 1) + [(0, 1)]), idx, axis=-1)


def _rope_tables(positions):
    t = positions.size
    rows = min(t, 2048)
    half = MLA_ROPE // 2
    inv = ROPE_THETA ** (-jnp.arange(half, dtype=F32) / half)
    rope_src = _head_lanes() - MLA_NOPE
    inv_lane = _place(jnp.concatenate([inv, inv]), np.where(rope_src >= 0, rope_src, -1))[None, :]
    sign_lane = _place(jnp.concatenate([-jnp.ones((half,), F32), jnp.ones((half,), F32)]),
                       np.where(rope_src >= 0, rope_src, -1))[None, :]
    row = pl.BlockSpec((rows, LANES), lambda i: (i, 0))
    const = pl.BlockSpec((1, LANES), lambda i: (0, 0))
    return pl.pallas_call(
        _rope_table_kernel,
        grid=(t // rows,),
        in_specs=[pl.BlockSpec((rows, 1), lambda i: (i, 0)), const, const],
        out_specs=[row, row],
        out_shape=[jax.ShapeDtypeStruct((t, LANES), F32)] * 2,
        compiler_params=_params("parallel"),
        name="rope_tables",
    )(positions.reshape(t, 1), inv_lane, sign_lane)


def _rope(v, c, s):
    out = []
    for h in range(v.shape[1] // HEAD_PAD):
        vb = v[:, h * HEAD_PAD:(h + 1) * HEAD_PAD]
        out.append(vb * c + pltpu.roll(vb, HEAD_PAD // 2, 1) * s)
    return out[0] if len(out) == 1 else jnp.concatenate(out, axis=1)


def _hyb_in_kernel(x_ref, g_ref, win_ref, convw_ref, qn_ref, kvn_ref, wuq_ref, wuk_ref,
                   wuv_ref, c_ref, s_ref, ya_ref, q_ref, k_ref, v_ref, u_sc, *, tiles_per_seq,
                   n_chains, q_scale):
    tm = x_ref.shape[0]
    rc = tm // n_chains
    cd, r = CONV_DIM, MLA_RANK

    @pl.when(pl.program_id(0) % tiles_per_seq == 0)
    def _():
        u_sc[0:8, :] = jnp.zeros((8, cd), F32)

    @pl.when(pl.program_id(0) % tiles_per_seq != 0)
    def _():
        u_sc[0:8, :] = u_sc[tm:tm + 8, :]

    chains = [slice(c * rc, (c + 1) * rc) for c in range(n_chains)]
    xns = [_rms(x_ref[rs, :], g_ref[...]).astype(BF16) for rs in chains]
    w = convw_ref[...]
    def down(rs, xn):
        b_gate = _dot(xn, win_ref[:, 0:cd])
        u = _dot(xn, win_ref[:, cd:2 * cd]) * _dot(xn, win_ref[:, 2 * cd:3 * cd])
        u_sc[8 + rs.start:8 + rs.stop, :] = u
        y = (w[2:3, :] * u + w[1:2, :] * u_sc[7 + rs.start:7 + rs.stop, :]
             + w[0:1, :] * u_sc[6 + rs.start:6 + rs.stop, :])
        ya_ref[rs, :] = (b_gate * y).astype(BF16)
        cq = _rms(_dot(xn, win_ref[:, 3 * cd:3 * cd + r]), qn_ref[...]).astype(BF16)
        ckv = _rms(_dot(xn, win_ref[:, 3 * cd + r:3 * cd + 2 * r]), kvn_ref[...]).astype(BF16)
        kr = _dot(xn, win_ref[:, 3 * cd + 2 * r:])
        return cq, ckv, kr

    def up(rs, cq, ckv, kr):
        c, s = c_ref[rs, :], s_ref[rs, :]
        q_ref[rs, :] = (_rope(_dot(cq, wuq_ref[...]), c, s) * q_scale).astype(BF16)
        k_rope = _rope(kr, c, s)
        k_ref[rs, :] = (_dot(ckv, wuk_ref[...])
                        + jnp.concatenate([k_rope] * MLA_HEADS, axis=1)).astype(BF16)
        v_ref[rs, :] = _dot(ckv, wuv_ref[...]).astype(BF16)

    low = None
    for c, (rs, xn) in enumerate(zip(chains, xns)):
        nxt = down(rs, xn)
        if low is not None:
            up(chains[c - 1], *low)
        low = nxt
    up(chains[-1], *low)


def _hyb_in(x, g, win, convw, qn, kvn, wuq, wuk, wuv, layer, c_tab, s_tab, seq, tm, n_chains):
    t = x.shape[0]
    hp = MLA_HEADS * HEAD_PAD
    row = lambda n: pl.BlockSpec((tm, n), lambda i: (i, 0))
    full = _resident
    stacked = lambda a: _resident(a, layer)
    return pl.pallas_call(
        functools.partial(_hyb_in_kernel, tiles_per_seq=seq // tm, n_chains=n_chains,
                          q_scale=ATTN_SCALE * LOG2_E),
        grid=(t // tm,),
        in_specs=[row(D_MODEL), full(g), stacked(win), full(convw), full(qn), full(kvn),
                  stacked(wuq), stacked(wuk), stacked(wuv), row(LANES), row(LANES)],
        out_specs=[row(CONV_DIM), row(hp), row(hp), row(MLA_HEADS * MLA_V)],
        out_shape=[jax.ShapeDtypeStruct((t, CONV_DIM), BF16),
                   jax.ShapeDtypeStruct((t, hp), BF16),
                   jax.ShapeDtypeStruct((t, hp), BF16),
                   jax.ShapeDtypeStruct((t, MLA_HEADS * MLA_V), BF16)],
        scratch_shapes=[pltpu.VMEM((tm + 8, CONV_DIM), F32)],
        compiler_params=_params("arbitrary"),
        name="hyb_in",
    )(x, g, win, convw, qn, kvn, wuq, wuk, wuv, c_tab, s_tab)


def _attn_kernel(q_ref, k_ref, v_ref, o_ref, *, blk):
    seq = q_ref.shape[0]
    lane = lax.broadcasted_iota(jnp.int32, (blk, 2 * MLA_V), 1)
    first = lane < MLA_V
    rows = lax.broadcasted_iota(jnp.int32, (blk, blk), 0) // CHUNK
    cols = lax.broadcasted_iota(jnp.int32, (blk, blk), 1) // CHUNK
    visible = cols <= rows

    def scores(i, h):
        q0 = i * blk
        hs = slice(h * HEAD_PAD, (h + 1) * HEAD_PAD)
        qh = q_ref[q0:q0 + blk, hs]
        s_diag = jnp.where(visible, _dot_nt(qh, k_ref[q0:q0 + blk, hs]), -jnp.inf)
        s_off = _dot_nt(qh, k_ref[0:q0, hs]) if i > 0 else None
        return s_diag, s_off

    def attend(i, s_diag, s_off):
        q0 = i * blk
        m = jnp.max(s_diag, axis=1, keepdims=True)
        if s_off is not None:
            m = jnp.maximum(m, jnp.max(s_off, axis=1, keepdims=True))
        p_diag = jnp.exp2(s_diag - m)
        den = jnp.sum(p_diag, axis=1, keepdims=True)
        pv = _dot(p_diag.astype(BF16), v_ref[q0:q0 + blk, :])
        if s_off is not None:
            p_off = jnp.exp2(s_off - m)
            den = den + jnp.sum(p_off, axis=1, keepdims=True)
            pv = pv + _dot(p_off.astype(BF16), v_ref[0:q0, :])
        return pv * (1.0 / den)

    units = [(i, h) for i in range(seq // blk) for h in range(2)]
    pending = scores(*units[0])
    outs = []
    for u, (i, h) in enumerate(units):
        ahead = scores(*units[u + 1]) if u + 1 < len(units) else None
        outs.append(attend(i, *pending))
        pending = ahead
        if h == 1:
            o_ref[i * blk:(i + 1) * blk, :] = jnp.where(first, outs[0], outs[1]).astype(o_ref.dtype)
            outs = []


def _attention(q, k, v, batch, seq, blk):
    t = q.shape[0]
    pairs = MLA_HEADS // 2
    spec = lambda n: pl.BlockSpec((seq, n), lambda b, p: (b, p))
    return pl.pallas_call(
        functools.partial(_attn_kernel, blk=blk),
        grid=(batch, pairs),
        in_specs=[spec(2 * HEAD_PAD), spec(2 * HEAD_PAD), spec(2 * MLA_V)],
        out_specs=spec(2 * MLA_V),
        out_shape=jax.ShapeDtypeStruct((t, MLA_HEADS * MLA_V), BF16),
        compiler_params=_params("parallel", "parallel"),
        name="mla_attention",
    )(q, k, v)


def _ml_in_kernel(x_ref, g_ref, win_ref, wvt_ref, wgate_ref, bias_ref, q_ref, k_ref, vt_ref,
                  o_ref, gate_ref, *, n_chains):
    rc = x_ref.shape[0] // n_chains
    d = D_MODEL
    chains = [slice(c * rc, (c + 1) * rc) for c in range(n_chains)]
    xns = [_rms(x_ref[rs, :], g_ref[...]).astype(BF16) for rs in chains]
    for rs, xn in zip(chains, xns):
        q_ref[rs, :] = (_dot(xn, win_ref[:, 0:d]) * (ML_DH ** -0.5)).astype(BF16)
        k_ref[rs, :] = _dot(xn, win_ref[:, d:2 * d]).astype(BF16)
        vt_ref[:, rs] = _dot_nt(wvt_ref[...], xn).astype(BF16)
        o_ref[rs, :] = jax.nn.sigmoid(_dot(xn, win_ref[:, 2 * d:3 * d])).astype(BF16)
        gate_ref[rs, :] = _dot(xn, wgate_ref[...]) + bias_ref[...]


def _ml_in(x, g, win, wvt, wgate, layer, bias, batch, seq, tm, n_chains):
    t = x.shape[0]
    tps = seq // tm
    row = lambda n: pl.BlockSpec((tm, n), lambda i: (i, 0))
    act = jax.ShapeDtypeStruct((t, D_MODEL), BF16)
    return pl.pallas_call(
        functools.partial(_ml_in_kernel, n_chains=n_chains),
        grid=(t // tm,),
        in_specs=[row(D_MODEL), _resident(g), _resident(win, layer), _resident(wvt, layer),
                  _resident(wgate, layer), _resident(bias)],
        out_specs=[row(D_MODEL), row(D_MODEL),
                   pl.BlockSpec((None, D_MODEL, tm), lambda i: (i // tps, 0, i % tps)),
                   row(D_MODEL), row(LANES)],
        out_shape=[act, act, jax.ShapeDtypeStruct((batch, D_MODEL, seq), BF16), act,
                   jax.ShapeDtypeStruct((t, LANES), F32)],
        compiler_params=_params("parallel"),
        name="mlstm_in",
    )(x, g, win, wvt, wgate, bias)


def _log_sigmoid(x):
    return jnp.minimum(x, 0.0) - jnp.log1p(jnp.exp(-jnp.abs(x)))


def _ml_cell_kernel(q_ref, k_ref, vt_ref, gate_ref, out_ref, st_sc, m_sc):
    nseq, ln = q_ref.shape[0], q_ref.shape[1]

    @pl.when(pl.program_id(1) == 0)
    def _():
        st_sc[...] = jnp.zeros(st_sc.shape, F32)
        m_sc[...] = jnp.zeros(m_sc.shape, F32)

    units = [(s, h) for s in range(nseq) for h in range(ML_HEADS)]
    ids = range(len(units))
    cols = lambda h: slice(h * ML_DH, (h + 1) * ML_DH)
    q = [q_ref[s, :, cols(h)] for s, h in units]
    k = [k_ref[s, :, cols(h)] for s, h in units]
    vt = [vt_ref[s, cols(h), :] for s, h in units]
    st_prev = [st_sc[u] for u in ids]
    kq = [_dot_nt(k[u], q[u]) for u in ids]
    carried = [_dot_nt(st_prev[u].astype(BF16), q[u]) for u in ids]

    r_idx = lax.broadcasted_iota(jnp.int32, (ln, ln), 0)
    c_idx = lax.broadcasted_iota(jnp.int32, (ln, ln), 1)
    visible = r_idx <= c_idx
    tril = jnp.where(c_idx <= r_idx, 1.0, 0.0).astype(BF16)
    first_row = lax.broadcasted_iota(jnp.int32, (ML_EXT, ln), 0) == 0
    gates, gates_t, bcum, bcum_t = [], [], [], []
    for s in range(nseq):
        g = gate_ref[s]
        lsig = _log_sigmoid(g)
        hi = lsig.astype(BF16)
        rest = lsig - hi.astype(F32)
        mid = rest.astype(BF16)
        low = (rest - mid.astype(F32)).astype(BF16)
        b = _dot(tril, hi) + _dot(tril, mid) + _dot(tril, low)
        gates.append(g)
        gates_t.append(g.T)
        bcum.append(b)
        bcum_t.append(b.T)

    scores, vtws, inters, recips, stats = [], [], [], [], []
    for u, (s, h) in enumerate(units):
        li_row = gates_t[s][h:h + 1, :]
        b_row = bcum_t[s][ML_HEADS + h:ML_HEADS + h + 1, :]
        bl_col = bcum[s][:, ML_HEADS + h:ML_HEADS + h + 1] - gates[s][:, h:h + 1]
        d_log = jnp.where(visible, b_row - bl_col, -jnp.inf)
        m_prev = m_sc[u][:, 0:1]
        m_t = jnp.maximum(b_row + m_prev, jnp.max(d_log, axis=0, keepdims=True))
        inter = jnp.exp(b_row + m_prev - m_t)
        sc = kq[u] * jnp.exp(d_log - m_t)
        den = inter * carried[u][ML_DH:ML_DH + 1, :] + jnp.sum(sc, axis=0, keepdims=True)
        recips.append(1.0 / jnp.maximum(jnp.abs(den), jnp.exp(-m_t)))
        scores.append(sc.astype(BF16))
        inters.append(inter)
        b_last = b_row[:, ln - 1:ln]
        a_row = b_last - b_row + li_row
        a_max = jnp.max(a_row, axis=1, keepdims=True)
        e_row = jnp.exp(a_row - a_max)
        vtws.append(jnp.concatenate([vt[u].astype(F32) * e_row, jnp.where(first_row, e_row, 0.0)],
                                    axis=0).astype(BF16))
        m_new = jnp.maximum(b_last + m_prev, a_max)
        stats.append((jnp.exp(b_last + m_prev - m_new), jnp.exp(a_max - m_new)))
        m_sc[u] = jnp.broadcast_to(m_new, (1, LANES))

    sv = [_dot(vt[u], scores[u]) for u in ids]
    local = [_dot(vtws[u], k[u]) for u in ids]

    for u, (s, h) in enumerate(units):
        num = inters[u] * carried[u][:ML_DH, :] + sv[u]
        out_ref[s, cols(h), :] = (num * recips[u]).astype(out_ref.dtype)
        sp, sn = stats[u]
        st_sc[u] = sp * st_prev[u] + sn * local[u]


def _ml_cell(q, k, vt, gates, batch, seq, ln, nseq):
    nc = seq // ln
    units = nseq * ML_HEADS
    view = lambda a: a.reshape(batch, seq, a.shape[-1])
    blk = lambda n: pl.BlockSpec((nseq, ln, n), lambda b, c: (b, c, 0))
    blk_t = pl.BlockSpec((nseq, D_MODEL, ln), lambda b, c: (b, 0, c))
    return pl.pallas_call(
        _ml_cell_kernel,
        grid=(batch // nseq, nc),
        in_specs=[blk(D_MODEL), blk(D_MODEL), blk_t, blk(LANES)],
        out_specs=blk_t,
        out_shape=jax.ShapeDtypeStruct((batch, D_MODEL, seq), BF16),
        scratch_shapes=[pltpu.VMEM((units, ML_DH + ML_EXT, ML_DH), F32),
                        pltpu.VMEM((units, 1, LANES), F32)],
        compiler_params=_params("parallel", "arbitrary"),
        name="mlstm_cell",
    )(view(q), view(k), vt, view(gates))


def _out_mlp_kernel(*refs, n_mix, gated, n_chains):
    mix = refs[:n_mix]
    wout_ref, x_ref, gpost_ref, gpre_ref, gmlp_ref, w1_ref, w2_ref, out_ref = refs[n_mix:]
    rc = x_ref.shape[0] // n_chains

    def mixed(rs):
        if not gated:
            return jnp.concatenate([p[rs, :] for p in mix], axis=1)
        cell_ref, gate_ref, hnorm_ref = mix
        heads = []
        for h in range(ML_HEADS):
            cs = slice(h * ML_DH, (h + 1) * ML_DH)
            ct = cell_ref[cs, rs].astype(F32)
            scale = lax.rsqrt(jnp.mean(ct * ct, axis=0, keepdims=True) + EPS)
            normed = (ct * scale).T * hnorm_ref[:, cs]
            heads.append((gate_ref[rs, cs].astype(F32) * normed).astype(BF16))
        return jnp.concatenate(heads, axis=1)

    x1s = []
    for c in range(n_chains):
        rs = slice(c * rc, (c + 1) * rc)
        y = _dot(mixed(rs), wout_ref[...])
        x1s.append(x_ref[rs, :] + _rms(y, gpost_ref[...]))
    for c in range(n_chains):
        rs = slice(c * rc, (c + 1) * rc)
        hn = _rms(x1s[c], gpre_ref[...]).astype(BF16)
        a = jnp.square(jnp.maximum(_dot(hn, w1_ref[...]), 0.0)).astype(BF16)
        out_ref[rs, :] = x1s[c] + _rms(_dot(a, w2_ref[...]), gmlp_ref[...])


def _out_mlp(mix, gated, wout, wout_layer, x, gpost, gpre, gmlp, w1, w2, mlp_layer, tm, n_chains):
    t = x.shape[0]
    row = lambda n: pl.BlockSpec((tm, n), lambda i: (i, 0))
    vec = _resident(gpost)
    if gated:
        tps = mix[0].shape[2] // tm
        mix_specs = [pl.BlockSpec((None, D_MODEL, tm), lambda i: (i // tps, 0, i % tps)),
                     row(D_MODEL), vec]
    else:
        mix_specs = [row(p.shape[1]) for p in mix]
    return pl.pallas_call(
        functools.partial(_out_mlp_kernel, n_mix=len(mix), gated=gated, n_chains=n_chains),
        grid=(t // tm,),
        in_specs=mix_specs
        + [_resident(wout, wout_layer), row(D_MODEL), vec, vec, vec,
           _resident(w1, mlp_layer), _resident(w2, mlp_layer)],
        out_specs=row(D_MODEL),
        out_shape=jax.ShapeDtypeStruct((t, D_MODEL), F32),
        compiler_params=_params("parallel"),
        name="out_mlp",
    )(*mix, wout, x, gpost, gpre, gmlp, w1, w2)


def _hyb_weights(w_in, w_uq, w_ukv):
    n = w_in.shape[0]
    split = 3 * CONV_DIM + 2 * MLA_RANK
    src = _head_lanes()
    heads = lambda w: w.astype(BF16).reshape(n, MLA_RANK, MLA_HEADS * HEAD_PAD)
    k_r = _place(w_in[:, :, split:], np.where(src >= MLA_NOPE, src - MLA_NOPE, -1))
    win = jnp.concatenate([w_in[:, :, :split], k_r], axis=2).astype(BF16)
    wuq = heads(_place(w_uq.reshape(n, MLA_RANK, MLA_HEADS, MLA_NOPE + MLA_ROPE), src))
    ukv = w_ukv.reshape(n, MLA_RANK, MLA_HEADS, MLA_NOPE + MLA_V)
    wuk = heads(_place(ukv[..., :MLA_NOPE], src))
    wuv = ukv[..., MLA_NOPE:].reshape(n, MLA_RANK, MLA_HEADS * MLA_V).astype(BF16)
    return win, wuq, wuk, wuv


def kernel(x, positions, norm_mix_pre, norm_mix_post, norm_mlp_pre, norm_mlp_post, hyb_w_in, conv_w, mla_q_norm, mla_kv_norm, mla_w_uq, mla_w_ukv, hyb_w_out, ml_w_in, ml_b_i, ml_b_f, ml_head_norm, ml_w_out, mlp_w1, mlp_w2):
    batch, seq, d = x.shape
    t = batch * seq
    depth = norm_mix_pre.shape[0]
    tm_in = min(1024, seq)
    in_chains = 4
    tm_mlp = min(1024, seq)
    mlp_chains = 4
    blk = min(256, seq)
    cell_seqs = 2 if batch % 2 == 0 else 1

    xt = x.reshape(t, d)
    c_tab, s_tab = _rope_tables(positions)
    vec = lambda a: a.reshape(1, -1)

    hyb_win, hyb_wuq, hyb_wuk, hyb_wuv = _hyb_weights(hyb_w_in, mla_w_uq, mla_w_ukv)
    hyb_wout = hyb_w_out.astype(BF16)
    ml_win = jnp.concatenate([ml_w_in[:, :, :2 * d], ml_w_in[:, :, 3 * d:4 * d]], axis=2).astype(BF16)
    ml_wvt = jnp.swapaxes(ml_w_in[:, :, 2 * d:3 * d], 1, 2).astype(BF16)
    ml_wgate = jnp.pad(ml_w_in[:, :, 4 * D_MODEL:].astype(BF16),
                       ((0, 0), (0, 0), (0, LANES - 2 * ML_HEADS)))
    ml_wout = ml_w_out.astype(BF16)
    w1 = mlp_w1.astype(BF16)
    w2 = mlp_w2.astype(BF16)

    for l in range(depth):
        e = l // 2
        if l % 2 == 0:
            y_a, q, k, v = _hyb_in(xt, vec(norm_mix_pre[l]), hyb_win, conv_w[e], vec(mla_q_norm[e]),
                                   vec(mla_kv_norm[e]), hyb_wuq, hyb_wuk, hyb_wuv, e, c_tab, s_tab,
                                   seq, tm_in, in_chains)
            y_b = _attention(q, k, v, batch, seq, blk)
            mix, wout = [y_a, y_b], hyb_wout
        else:
            bias = jnp.pad(jnp.concatenate([ml_b_i[e], ml_b_f[e]]), (0, LANES - 2 * ML_HEADS))[None, :]
            q, k, vt, o_gate, gates = _ml_in(xt, vec(norm_mix_pre[l]), ml_win, ml_wvt, ml_wgate, e,
                                             bias, batch, seq, tm_in, in_chains)
            cell = _ml_cell(q, k, vt, gates, batch, seq, blk, cell_seqs)
            mix, wout = [cell, o_gate, vec(ml_head_norm[e])], ml_wout
        xt = _out_mlp(mix, l % 2 == 1, wout, e, xt, vec(norm_mix_post[l]), vec(norm_mlp_pre[l]),
                      vec(norm_mlp_post[l]), w1, w2, l, tm_mlp, mlp_chains)
    return xt.reshape(batch, seq, d)
```

```python
import functools

import jax
import jax.numpy as jnp
import numpy as np
from jax import lax
from jax.experimental import pallas as pl
from jax.experimental.pallas import tpu as pltpu

D_MODEL = 1024
EPS = 1e-6
CHUNK = 64
CONV_DIM = 512
CONV_WIDTH = 3
MLA_HEADS = 8
MLA_RANK = 256
MLA_NOPE = 64
MLA_ROPE = 32
MLA_V = 64
ROPE_THETA = 10000.0
ML_HEADS = 4
ML_DH = 256
D_FF = 4096

ATTN_SCALE = (MLA_NOPE + MLA_ROPE) ** -0.5
LOG2_E = 1.4426950408889634

LANES = 128
HEAD_PAD = LANES
ML_EXT = 16
VMEM_LIMIT = 56 * 1024 * 1024

BF16 = jnp.bfloat16
F32 = jnp.float32


def _dot(a, b):
    return jnp.dot(a, b, preferred_element_type=F32)


def _dot_nt(a, b):
    return lax.dot_general(a, b, (((1,), (1,)), ((), ())), preferred_element_type=F32)


def _rms(x, g):
    return x * lax.rsqrt(jnp.mean(x * x, axis=-1, keepdims=True) + EPS) * g


def _params(*semantics):
    return pltpu.CompilerParams(dimension_semantics=semantics, vmem_limit_bytes=VMEM_LIMIT)


def _resident(a, layer=None):
    if layer is None:
        return pl.BlockSpec(a.shape, lambda *_: (0,) * a.ndim, pipeline_mode=pl.Buffered(1))
    return pl.BlockSpec((None,) + a.shape[1:], lambda *_: (layer,) + (0,) * (a.ndim - 1),
                        pipeline_mode=pl.Buffered(1))


def _cast_plan(w, layer, axis, steps):
    r, c = w.shape[1:]
    if axis == 0:
        blk, imap_in, imap_out = (r // steps, c), (lambda i: (layer, i, 0)), (lambda i: (i, 0))
    else:
        blk, imap_in, imap_out = (r, c // steps), (lambda i: (layer, 0, i)), (lambda i: (0, i))
    return (pl.BlockSpec((None,) + blk, imap_in), pl.BlockSpec(blk, imap_out),
            jax.ShapeDtypeStruct((r, c), BF16))


def _cast_slices(refs):
    n = len(refs) // 2
    for src, dst in zip(refs[:n], refs[n:]):
        dst[...] = src[...].astype(BF16)


def _rope_table_kernel(pos_ref, inv_ref, cos_ref, sin_ref):
    ang = pos_ref[...].astype(F32) * inv_ref[...]
    cos_ref[...] = jnp.cos(ang)
    sin_ref[...] = jnp.sin(ang)


def _head_lanes():
    half = MLA_ROPE // 2
    x1 = LANES // 2 - half
    src = -np.ones((LANES,), np.int32)
    src[:x1] = np.arange(x1)
    src[x1:x1 + half] = MLA_NOPE + np.arange(half)
    src[LANES // 2:LANES // 2 + MLA_NOPE - x1] = x1 + np.arange(MLA_NOPE - x1)
    src[LANES - half:] = MLA_NOPE + half + np.arange(half)
    return src


def _place(w, src):
    n = w.shape[-1]
    idx = np.where((src >= 0) & (src < n), src, n)
    return jnp.take(jnp.pad(w, [(0, 0)] * (w.ndim - 1) + [(0, 1)]), idx, axis=-1)


def _rope_tables(positions):
    t = positions.size
    half = MLA_ROPE // 2
    per_row = LANES // half
    inv = ROPE_THETA ** (-jnp.arange(half, dtype=F32) / half)
    pos = jnp.repeat(positions.reshape(t // per_row, per_row), half, axis=1)
    cos, sin = pl.pallas_call(
        _rope_table_kernel,
        out_shape=[jax.ShapeDtypeStruct((t // per_row, LANES), F32)] * 2,
        name="rope_tables",
    )(pos, jnp.tile(inv, per_row)[None, :])
    cos, sin = cos.reshape(t, half), sin.reshape(t, half)
    rope_src = _head_lanes() - MLA_NOPE
    on_rope = jnp.asarray(rope_src >= 0)
    c_tab = jnp.where(on_rope, _place(jnp.concatenate([cos, cos], axis=1), rope_src), 1.0)
    s_tab = _place(jnp.concatenate([-sin, sin], axis=1), rope_src)
    return c_tab, s_tab


def _rope(v, c, s):
    out = []
    for h in range(v.shape[1] // HEAD_PAD):
        vb = v[:, h * HEAD_PAD:(h + 1) * HEAD_PAD]
        out.append(vb * c + pltpu.roll(vb, HEAD_PAD // 2, 1) * s)
    return out[0] if len(out) == 1 else jnp.concatenate(out, axis=1)


def _hyb_in_kernel(x_ref, g_ref, win_ref, convw_ref, qn_ref, kvn_ref, wuq_ref, wuk_ref,
                   wuv_ref, c_ref, s_ref, *rest, tiles_per_seq, n_chains, q_scale):
    n_cast = (len(rest) - 5) // 2
    ya_ref, q_ref, k_ref, v_ref = rest[n_cast:n_cast + 4]
    u_sc = rest[-1]
    _cast_slices(rest[:n_cast] + rest[n_cast + 4:-1])
    tm = x_ref.shape[0]
    rc = tm // n_chains
    cd, r = CONV_DIM, MLA_RANK

    @pl.when(pl.program_id(0) % tiles_per_seq == 0)
    def _():
        u_sc[0:8, :] = jnp.zeros((8, cd), F32)

    @pl.when(pl.program_id(0) % tiles_per_seq != 0)
    def _():
        u_sc[0:8, :] = u_sc[tm:tm + 8, :]

    chains = [slice(c * rc, (c + 1) * rc) for c in range(n_chains)]
    xns = [_rms(x_ref[rs, :], g_ref[...]).astype(BF16) for rs in chains]
    w = convw_ref[...]
    def down(rs, xn):
        b_gate = _dot(xn, win_ref[:, 0:cd])
        u = _dot(xn, win_ref[:, cd:2 * cd]) * _dot(xn, win_ref[:, 2 * cd:3 * cd])
        u_sc[8 + rs.start:8 + rs.stop, :] = u
        y = (w[2:3, :] * u + w[1:2, :] * u_sc[7 + rs.start:7 + rs.stop, :]
             + w[0:1, :] * u_sc[6 + rs.start:6 + rs.stop, :])
        ya_ref[rs, :] = (b_gate * y).astype(BF16)
        cq = _rms(_dot(xn, win_ref[:, 3 * cd:3 * cd + r]), qn_ref[...]).astype(BF16)
        ckv = _rms(_dot(xn, win_ref[:, 3 * cd + r:3 * cd + 2 * r]), kvn_ref[...]).astype(BF16)
        kr = _dot(xn, win_ref[:, 3 * cd + 2 * r:])
        return cq, ckv, kr

    def up(rs, cq, ckv, kr):
        c, s = c_ref[rs, :], s_ref[rs, :]
        q_ref[rs, :] = (_rope(_dot(cq, wuq_ref[...]), c, s) * q_scale).astype(BF16)
        k_rope = _rope(kr, c, s)
        k_ref[rs, :] = (_dot(ckv, wuk_ref[...])
                        + jnp.concatenate([k_rope] * MLA_HEADS, axis=1)).astype(BF16)
        v_ref[rs, :] = _dot(ckv, wuv_ref[...]).astype(BF16)

    low = None
    for c, (rs, xn) in enumerate(zip(chains, xns)):
        nxt = down(rs, xn)
        if low is not None:
            up(chains[c - 1], *low)
        low = nxt
    up(chains[-1], *low)


def _hyb_in(x, g, win, convw, qn, kvn, wuq, wuk, wuv, layer, c_tab, s_tab, casts, seq, tm,
            n_chains):
    t = x.shape[0]
    hp = MLA_HEADS * HEAD_PAD
    row = lambda n: pl.BlockSpec((tm, n), lambda i: (i, 0))
    full = _resident
    stacked = lambda a: _resident(a, layer)
    plans = [_cast_plan(w, l, axis, t // tm) for w, l, axis in casts]
    return pl.pallas_call(
        functools.partial(_hyb_in_kernel, tiles_per_seq=seq // tm, n_chains=n_chains,
                          q_scale=ATTN_SCALE * LOG2_E),
        grid=(t // tm,),
        in_specs=[row(D_MODEL), full(g), stacked(win), full(convw), full(qn), full(kvn),
                  stacked(wuq), stacked(wuk), stacked(wuv), row(LANES), row(LANES)]
        + [p[0] for p in plans],
        out_specs=[row(CONV_DIM), row(hp), row(hp), row(MLA_HEADS * MLA_V)] + [p[1] for p in plans],
        out_shape=[jax.ShapeDtypeStruct((t, CONV_DIM), BF16),
                   jax.ShapeDtypeStruct((t, hp), BF16),
                   jax.ShapeDtypeStruct((t, hp), BF16),
                   jax.ShapeDtypeStruct((t, MLA_HEADS * MLA_V), BF16)] + [p[2] for p in plans],
        scratch_shapes=[pltpu.VMEM((tm + 8, CONV_DIM), F32)],
        compiler_params=_params("arbitrary"),
        name="hyb_in",
    )(x, g, win, convw, qn, kvn, wuq, wuk, wuv, c_tab, s_tab, *[w for w, _, _ in casts])


def _attn_kernel(q_ref, k_ref, v_ref, o_ref, *, blk):
    seq = q_ref.shape[0]
    lane = lax.broadcasted_iota(jnp.int32, (blk, 2 * MLA_V), 1)
    first = lane < MLA_V
    rows = lax.broadcasted_iota(jnp.int32, (blk, blk), 0) // CHUNK
    cols = lax.broadcasted_iota(jnp.int32, (blk, blk), 1) // CHUNK
    visible = cols <= rows

    def scores(i, h):
        q0 = i * blk
        hs = slice(h * HEAD_PAD, (h + 1) * HEAD_PAD)
        qh = q_ref[q0:q0 + blk, hs]
        s_diag = jnp.where(visible, _dot_nt(qh, k_ref[q0:q0 + blk, hs]), -jnp.inf)
        s_off = _dot_nt(qh, k_ref[0:q0, hs]) if i > 0 else None
        return s_diag, s_off

    def attend(i, s_diag, s_off):
        q0 = i * blk
        m = jnp.max(s_diag, axis=1, keepdims=True)
        if s_off is not None:
            m = jnp.maximum(m, jnp.max(s_off, axis=1, keepdims=True))
        p_diag = jnp.exp2(s_diag - m)
        den = jnp.sum(p_diag, axis=1, keepdims=True)
        pv = _dot(p_diag.astype(BF16), v_ref[q0:q0 + blk, :])
        if s_off is not None:
            p_off = jnp.exp2(s_off - m)
            den = den + jnp.sum(p_off, axis=1, keepdims=True)
            pv = pv + _dot(p_off.astype(BF16), v_ref[0:q0, :])
        return pv * (1.0 / den)

    units = [(i, h) for i in range(seq // blk) for h in range(2)]
    pending = scores(*units[0])
    outs = []
    for u, (i, h) in enumerate(units):
        ahead = scores(*units[u + 1]) if u + 1 < len(units) else None
        outs.append(attend(i, *pending))
        pending = ahead
        if h == 1:
            o_ref[i * blk:(i + 1) * blk, :] = jnp.where(first, outs[0], outs[1]).astype(o_ref.dtype)
            outs = []


def _attention(q, k, v, batch, seq, blk):
    t = q.shape[0]
    pairs = MLA_HEADS // 2
    spec = lambda n: pl.BlockSpec((seq, n), lambda b, p: (b, p))
    return pl.pallas_call(
        functools.partial(_attn_kernel, blk=blk),
        grid=(batch, pairs),
        in_specs=[spec(2 * HEAD_PAD), spec(2 * HEAD_PAD), spec(2 * MLA_V)],
        out_specs=spec(2 * MLA_V),
        out_shape=jax.ShapeDtypeStruct((t, MLA_HEADS * MLA_V), BF16),
        compiler_params=_params("parallel", "parallel"),
        name="mla_attention",
    )(q, k, v)


def _ml_in_kernel(x_ref, g_ref, win_ref, wgate_ref, bias_ref, *rest, n_chains):
    n_cast = (len(rest) - 6) // 2
    q_ref, k_ref, vt_ref, o_ref, gate_ref = rest[n_cast:n_cast + 5]
    wvt_sc = rest[-1]
    _cast_slices(rest[:n_cast] + rest[n_cast + 5:-1])
    rc = x_ref.shape[0] // n_chains
    d = D_MODEL

    @pl.when(pl.program_id(0) == 0)
    def _():
        for j in range(d // ML_DH):
            cs = slice(j * ML_DH, (j + 1) * ML_DH)
            wvt_sc[cs, :] = win_ref[:, 2 * d + cs.start:2 * d + cs.stop].astype(F32).T.astype(BF16)

    chains = [slice(c * rc, (c + 1) * rc) for c in range(n_chains)]
    xns = [_rms(x_ref[rs, :], g_ref[...]).astype(BF16) for rs in chains]
    for rs, xn in zip(chains, xns):
        q_ref[rs, :] = (_dot(xn, win_ref[:, 0:d]) * (ML_DH ** -0.5)).astype(BF16)
        k_ref[rs, :] = _dot(xn, win_ref[:, d:2 * d]).astype(BF16)
        vt_ref[:, rs] = _dot_nt(wvt_sc[...], xn).astype(BF16)
        o_ref[rs, :] = jax.nn.sigmoid(_dot(xn, win_ref[:, 3 * d:4 * d])).astype(BF16)
        gate_ref[rs, :] = _dot(xn, wgate_ref[...]) + bias_ref[...]


def _ml_in(x, g, win, wgate, layer, bias, casts, batch, seq, tm, n_chains):
    t = x.shape[0]
    tps = seq // tm
    row = lambda n: pl.BlockSpec((tm, n), lambda i: (i, 0))
    act = jax.ShapeDtypeStruct((t, D_MODEL), BF16)
    plans = [_cast_plan(w, l, axis, t // tm) for w, l, axis in casts]
    return pl.pallas_call(
        functools.partial(_ml_in_kernel, n_chains=n_chains),
        grid=(t // tm,),
        in_specs=[row(D_MODEL), _resident(g), _resident(win, layer), _resident(wgate, layer),
                  _resident(bias)] + [p[0] for p in plans],
        out_specs=[row(D_MODEL), row(D_MODEL),
                   pl.BlockSpec((None, D_MODEL, tm), lambda i: (i // tps, 0, i % tps)),
                   row(D_MODEL), row(LANES)] + [p[1] for p in plans],
        out_shape=[act, act, jax.ShapeDtypeStruct((batch, D_MODEL, seq), BF16), act,
                   jax.ShapeDtypeStruct((t, LANES), F32)] + [p[2] for p in plans],
        scratch_shapes=[pltpu.VMEM((D_MODEL, D_MODEL), BF16)],
        compiler_params=_params("arbitrary"),
        name="mlstm_in",
    )(x, g, win, wgate, bias, *[w for w, _, _ in casts])


def _log_sigmoid(x):
    return jnp.minimum(x, 0.0) - jnp.log1p(jnp.exp(-jnp.abs(x)))


def _ml_cell_kernel(q_ref, k_ref, vt_ref, gate_ref, out_ref, st_sc, m_sc):
    nseq, ln = q_ref.shape[0], q_ref.shape[1]

    @pl.when(pl.program_id(1) == 0)
    def _():
        st_sc[...] = jnp.zeros(st_sc.shape, F32)
        m_sc[...] = jnp.zeros(m_sc.shape, F32)

    units = [(s, h) for s in range(nseq) for h in range(ML_HEADS)]
    ids = range(len(units))
    cols = lambda h: slice(h * ML_DH, (h + 1) * ML_DH)
    q = [q_ref[s, :, cols(h)] for s, h in units]
    k = [k_ref[s, :, cols(h)] for s, h in units]
    vt = [vt_ref[s, cols(h), :] for s, h in units]
    st_prev = [st_sc[u] for u in ids]
    kq = [_dot_nt(k[u], q[u]) for u in ids]
    carried = [_dot_nt(st_prev[u].astype(BF16), q[u]) for u in ids]

    r_idx = lax.broadcasted_iota(jnp.int32, (ln, ln), 0)
    c_idx = lax.broadcasted_iota(jnp.int32, (ln, ln), 1)
    visible = r_idx <= c_idx
    tril = jnp.where(c_idx <= r_idx, 1.0, 0.0).astype(BF16)
    first_row = lax.broadcasted_iota(jnp.int32, (ML_EXT, ln), 0) == 0
    gates, gates_t, bcum, bcum_t = [], [], [], []
    for s in range(nseq):
        g = gate_ref[s]
        lsig = _log_sigmoid(g)
        hi = lsig.astype(BF16)
        rest = lsig - hi.astype(F32)
        mid = rest.astype(BF16)
        low = (rest - mid.astype(F32)).astype(BF16)
        b = _dot(tril, hi) + _dot(tril, mid) + _dot(tril, low)
        gates.append(g)
        gates_t.append(g.T)
        bcum.append(b)
        bcum_t.append(b.T)

    scores, vtws, inters, recips, stats = [], [], [], [], []
    for u, (s, h) in enumerate(units):
        li_row = gates_t[s][h:h + 1, :]
        b_row = bcum_t[s][ML_HEADS + h:ML_HEADS + h + 1, :]
        bl_col = bcum[s][:, ML_HEADS + h:ML_HEADS + h + 1] - gates[s][:, h:h + 1]
        d_log = jnp.where(visible, b_row - bl_col, -jnp.inf)
        m_prev = m_sc[u][:, 0:1]
        m_t = jnp.maximum(b_row + m_prev, jnp.max(d_log, axis=0, keepdims=True))
        inter = jnp.exp(b_row + m_prev - m_t)
        sc = kq[u] * jnp.exp(d_log - m_t)
        den = inter * carried[u][ML_DH:ML_DH + 1, :] + jnp.sum(sc, axis=0, keepdims=True)
        recips.append(1.0 / jnp.maximum(jnp.abs(den), jnp.exp(-m_t)))
        scores.append(sc.astype(BF16))
        inters.append(inter)
        b_last = b_row[:, ln - 1:ln]
        a_row = b_last - b_row + li_row
        a_max = jnp.max(a_row, axis=1, keepdims=True)
        e_row = jnp.exp(a_row - a_max)
        vtws.append(jnp.concatenate([vt[u].astype(F32) * e_row, jnp.where(first_row, e_row, 0.0)],
                                    axis=0).astype(BF16))
        m_new = jnp.maximum(b_last + m_prev, a_max)
        stats.append((jnp.exp(b_last + m_prev - m_new), jnp.exp(a_max - m_new)))
        m_sc[u] = jnp.broadcast_to(m_new, (1, LANES))

    sv = [_dot(vt[u], scores[u]) for u in ids]
    local = [_dot(vtws[u], k[u]) for u in ids]

    for u, (s, h) in enumerate(units):
        num = inters[u] * carried[u][:ML_DH, :] + sv[u]
        out_ref[s, cols(h), :] = (num * recips[u]).astype(out_ref.dtype)
        sp, sn = stats[u]
        st_sc[u] = sp * st_prev[u] + sn * local[u]


def _ml_cell(q, k, vt, gates, batch, seq, ln, nseq):
    nc = seq // ln
    units = nseq * ML_HEADS
    view = lambda a: a.reshape(batch, seq, a.shape[-1])
    blk = lambda n: pl.BlockSpec((nseq, ln, n), lambda b, c: (b, c, 0))
    blk_t = pl.BlockSpec((nseq, D_MODEL, ln), lambda b, c: (b, 0, c))
    return pl.pallas_call(
        _ml_cell_kernel,
        grid=(batch // nseq, nc),
        in_specs=[blk(D_MODEL), blk(D_MODEL), blk_t, blk(LANES)],
        out_specs=blk_t,
        out_shape=jax.ShapeDtypeStruct((batch, D_MODEL, seq), BF16),
        scratch_shapes=[pltpu.VMEM((units, ML_DH + ML_EXT, ML_DH), F32),
                        pltpu.VMEM((units, 1, LANES), F32)],
        compiler_params=_params("parallel", "arbitrary"),
        name="mlstm_cell",
    )(view(q), view(k), vt, view(gates))


def _out_mlp_kernel(*refs, n_mix, gated, n_chains):
    mix = refs[:n_mix]
    wout_ref, x_ref, gpost_ref, gpre_ref, gmlp_ref, w1_ref, w2_ref, out_ref = refs[n_mix:]
    rc = x_ref.shape[0] // n_chains

    def mixed(rs):
        if not gated:
            return jnp.concatenate([p[rs, :] for p in mix], axis=1)
        cell_ref, gate_ref, hnorm_ref = mix
        heads = []
        for h in range(ML_HEADS):
            cs = slice(h * ML_DH, (h + 1) * ML_DH)
            ct = cell_ref[cs, rs].astype(F32)
            scale = lax.rsqrt(jnp.mean(ct * ct, axis=0, keepdims=True) + EPS)
            normed = (ct * scale).T * hnorm_ref[:, cs]
            heads.append((gate_ref[rs, cs].astype(F32) * normed).astype(BF16))
        return jnp.concatenate(heads, axis=1)

    x1s = []
    for c in range(n_chains):
        rs = slice(c * rc, (c + 1) * rc)
        y = _dot(mixed(rs), wout_ref[...])
        x1s.append(x_ref[rs, :] + _rms(y, gpost_ref[...]))
    for c in range(n_chains):
        rs = slice(c * rc, (c + 1) * rc)
        hn = _rms(x1s[c], gpre_ref[...]).astype(BF16)
        a = jnp.square(jnp.maximum(_dot(hn, w1_ref[...]), 0.0)).astype(BF16)
        out_ref[rs, :] = x1s[c] + _rms(_dot(a, w2_ref[...]), gmlp_ref[...])


def _out_mlp(mix, gated, wout, x, gpost, gpre, gmlp, w1, w2, tm, n_chains):
    t = x.shape[0]
    row = lambda n: pl.BlockSpec((tm, n), lambda i: (i, 0))
    vec = _resident(gpost)
    if gated:
        tps = mix[0].shape[2] // tm
        mix_specs = [pl.BlockSpec((None, D_MODEL, tm), lambda i: (i // tps, 0, i % tps)),
                     row(D_MODEL), vec]
    else:
        mix_specs = [row(p.shape[1]) for p in mix]
    return pl.pallas_call(
        functools.partial(_out_mlp_kernel, n_mix=len(mix), gated=gated, n_chains=n_chains),
        grid=(t // tm,),
        in_specs=mix_specs
        + [_resident(wout), row(D_MODEL), vec, vec, vec, _resident(w1), _resident(w2)],
        out_specs=row(D_MODEL),
        out_shape=jax.ShapeDtypeStruct((t, D_MODEL), F32),
        compiler_params=_params("parallel"),
        name="out_mlp",
    )(*mix, wout, x, gpost, gpre, gmlp, w1, w2)


def _hyb_weights(w_in, w_uq, w_ukv):
    n = w_in.shape[0]
    split = 3 * CONV_DIM + 2 * MLA_RANK
    src = _head_lanes()
    heads = lambda w: w.astype(BF16).reshape(n, MLA_RANK, MLA_HEADS * HEAD_PAD)
    k_r = _place(w_in[:, :, split:], np.where(src >= MLA_NOPE, src - MLA_NOPE, -1))
    win = jnp.concatenate([w_in[:, :, :split], k_r], axis=2).astype(BF16)
    wuq = heads(_place(w_uq.reshape(n, MLA_RANK, MLA_HEADS, MLA_NOPE + MLA_ROPE), src))
    ukv = w_ukv.reshape(n, MLA_RANK, MLA_HEADS, MLA_NOPE + MLA_V)
    wuk = heads(_place(ukv[..., :MLA_NOPE], src))
    wuv = ukv[..., MLA_NOPE:].reshape(n, MLA_RANK, MLA_HEADS * MLA_V).astype(BF16)
    return win, wuq, wuk, wuv


def kernel(x, positions, norm_mix_pre, norm_mix_post, norm_mlp_pre, norm_mlp_post, hyb_w_in, conv_w, mla_q_norm, mla_kv_norm, mla_w_uq, mla_w_ukv, hyb_w_out, ml_w_in, ml_b_i, ml_b_f, ml_head_norm, ml_w_out, mlp_w1, mlp_w2):
    batch, seq, d = x.shape
    t = batch * seq
    depth = norm_mix_pre.shape[0]
    tm_in = min(1024, seq)
    in_chains = 4
    tm_mlp = min(1024, seq)
    mlp_chains = 4
    blk = min(256, seq)
    cell_seqs = 2 if batch % 2 == 0 else 1

    xt = x.reshape(t, d)
    c_tab, s_tab = _rope_tables(positions)
    vec = lambda a: a.reshape(1, -1)

    hyb_win, hyb_wuq, hyb_wuk, hyb_wuv = _hyb_weights(hyb_w_in, mla_w_uq, mla_w_ukv)
    ml_win = ml_w_in[:, :, :4 * d].astype(BF16)
    ml_wgate = jnp.pad(ml_w_in[:, :, 4 * D_MODEL:].astype(BF16),
                       ((0, 0), (0, 0), (0, LANES - 2 * ML_HEADS)))

    for l in range(depth):
        e = l // 2
        casts = [(hyb_w_out if l % 2 == 0 else ml_w_out, e, 0), (mlp_w1, l, 1), (mlp_w2, l, 0)]
        if l % 2 == 0:
            y_a, q, k, v, wout, w1, w2 = _hyb_in(
                xt, vec(norm_mix_pre[l]), hyb_win, conv_w[e], vec(mla_q_norm[e]),
                vec(mla_kv_norm[e]), hyb_wuq, hyb_wuk, hyb_wuv, e, c_tab, s_tab, casts,
                seq, tm_in, in_chains)
            y_b = _attention(q, k, v, batch, seq, blk)
            mix = [y_a, y_b]
        else:
            bias = jnp.pad(jnp.concatenate([ml_b_i[e], ml_b_f[e]]), (0, LANES - 2 * ML_HEADS))[None, :]
            q, k, vt, o_gate, gates, wout, w1, w2 = _ml_in(
                xt, vec(norm_mix_pre[l]), ml_win, ml_wgate, e, bias, casts,
                batch, seq, tm_in, in_chains)
            cell = _ml_cell(q, k, vt, gates, batch, seq, blk, cell_seqs)
            mix = [cell, o_gate, vec(ml_head_norm[e])]
        xt = _out_mlp(mix, l % 2 == 1, wout, xt, vec(norm_mix_post[l]), vec(norm_mlp_pre[l]),
                      vec(norm_mlp_post[l]), w1, w2, tm_mlp, mlp_chains)
    return xt.reshape(batch, seq, d)
```

```python
import functools

import jax
import jax.numpy as jnp
import numpy as np
from jax import lax
from jax.experimental import pallas as pl
from jax.experimental.pallas import tpu as pltpu

D_MODEL = 1024
EPS = 1e-6
CHUNK = 64
CONV_DIM = 512
CONV_WIDTH = 3
MLA_HEADS = 8
MLA_RANK = 256
MLA_NOPE = 64
MLA_ROPE = 32
MLA_V = 64
ROPE_THETA = 10000.0
ML_HEADS = 4
ML_DH = 256
D_FF = 4096

ATTN_SCALE = (MLA_NOPE + MLA_ROPE) ** -0.5
LOG2_E = 1.4426950408889634

LANES = 128
HEAD_PAD = LANES
ML_EXT = 16
VMEM_LIMIT = 56 * 1024 * 1024

BF16 = jnp.bfloat16
F32 = jnp.float32


def _dot(a, b):
    return jnp.dot(a, b, preferred_element_type=F32)


def _dot_nt(a, b):
    return lax.dot_general(a, b, (((1,), (1,)), ((), ())), preferred_element_type=F32)


def _rms(x, g):
    return x * lax.rsqrt(jnp.mean(x * x, axis=-1, keepdims=True) + EPS) * g


def _params(*semantics):
    return pltpu.CompilerParams(dimension_semantics=semantics, vmem_limit_bytes=VMEM_LIMIT)


def _resident(a, layer=None):
    if layer is None:
        return pl.BlockSpec(a.shape, lambda *_: (0,) * a.ndim, pipeline_mode=pl.Buffered(1))
    return pl.BlockSpec((None,) + a.shape[1:], lambda *_: (layer,) + (0,) * (a.ndim - 1),
                        pipeline_mode=pl.Buffered(1))


def _cast_plan(w, layer, axis, steps):
    r, c = w.shape[1:]
    if axis == 0:
        blk, imap_in, imap_out = (r // steps, c), (lambda i: (layer, i, 0)), (lambda i: (i, 0))
    else:
        blk, imap_in, imap_out = (r, c // steps), (lambda i: (layer, 0, i)), (lambda i: (0, i))
    return (pl.BlockSpec((None,) + blk, imap_in), pl.BlockSpec(blk, imap_out),
            jax.ShapeDtypeStruct((r, c), BF16))


def _cast_slices(refs):
    n = len(refs) // 2
    for src, dst in zip(refs[:n], refs[n:]):
        dst[...] = src[...].astype(BF16)


def _rope_table_kernel(pos_ref, inv_ref, cos_ref, sin_ref):
    ang = pos_ref[...].astype(F32) * inv_ref[...]
    cos_ref[...] = jnp.cos(ang)
    sin_ref[...] = jnp.sin(ang)


def _head_lanes():
    half = MLA_ROPE // 2
    x1 = LANES // 2 - half
    src = -np.ones((LANES,), np.int32)
    src[:x1] = np.arange(x1)
    src[x1:x1 + half] = MLA_NOPE + np.arange(half)
    src[LANES // 2:LANES // 2 + MLA_NOPE - x1] = x1 + np.arange(MLA_NOPE - x1)
    src[LANES - half:] = MLA_NOPE + half + np.arange(half)
    return src


def _place(w, src):
    n = w.shape[-1]
    idx = np.where((src >= 0) & (src < n), src, n)
    return jnp.take(jnp.pad(w, [(0, 0)] * (w.ndim - 1) + [(0, 1)]), idx, axis=-1)


def _rope_tables(positions):
    t = positions.size
    half = MLA_ROPE // 2
    per_row = LANES // half
    inv = ROPE_THETA ** (-jnp.arange(half, dtype=F32) / half)
    pos = jnp.repeat(positions.reshape(t // per_row, per_row), half, axis=1)
    cos, sin = pl.pallas_call(
        _rope_table_kernel,
        out_shape=[jax.ShapeDtypeStruct((t // per_row, LANES), F32)] * 2,
        name="rope_tables",
    )(pos, jnp.tile(inv, per_row)[None, :])
    cos, sin = cos.reshape(t, half), sin.reshape(t, half)
    rope_src = _head_lanes() - MLA_NOPE
    on_rope = jnp.asarray(rope_src >= 0)
    c_tab = jnp.where(on_rope, _place(jnp.concatenate([cos, cos], axis=1), rope_src), 1.0)
    s_tab = _place(jnp.concatenate([-sin, sin], axis=1), rope_src)
    return c_tab, s_tab


def _rope(v, c, s):
    out = []
    for h in range(v.shape[1] // HEAD_PAD):
        vb = v[:, h * HEAD_PAD:(h + 1) * HEAD_PAD]
        out.append(vb * c + pltpu.roll(vb, HEAD_PAD // 2, 1) * s)
    return out[0] if len(out) == 1 else jnp.concatenate(out, axis=1)


def _hyb_in_kernel(x_ref, g_ref, win_ref, convw_ref, qn_ref, kvn_ref, wuq_ref, wuk_ref,
                   wuv_ref, c_ref, s_ref, *rest, tiles_per_seq, n_chains, q_scale):
    n_cast = (len(rest) - 5) // 2
    ya_ref, q_ref, k_ref, v_ref = rest[n_cast:n_cast + 4]
    u_sc = rest[-1]
    _cast_slices(rest[:n_cast] + rest[n_cast + 4:-1])
    tm = x_ref.shape[0]
    rc = tm // n_chains
    cd, r = CONV_DIM, MLA_RANK

    @pl.when(pl.program_id(0) % tiles_per_seq == 0)
    def _():
        u_sc[0:8, :] = jnp.zeros((8, cd), F32)

    @pl.when(pl.program_id(0) % tiles_per_seq != 0)
    def _():
        u_sc[0:8, :] = u_sc[tm:tm + 8, :]

    chains = [slice(c * rc, (c + 1) * rc) for c in range(n_chains)]
    xns = [_rms(x_ref[rs, :], g_ref[...]).astype(BF16) for rs in chains]
    w = convw_ref[...]
    def down(rs, xn):
        b_gate = _dot(xn, win_ref[:, 0:cd])
        u = _dot(xn, win_ref[:, cd:2 * cd]) * _dot(xn, win_ref[:, 2 * cd:3 * cd])
        u_sc[8 + rs.start:8 + rs.stop, :] = u
        y = (w[2:3, :] * u + w[1:2, :] * u_sc[7 + rs.start:7 + rs.stop, :]
             + w[0:1, :] * u_sc[6 + rs.start:6 + rs.stop, :])
        ya_ref[rs, :] = (b_gate * y).astype(BF16)
        cq = _rms(_dot(xn, win_ref[:, 3 * cd:3 * cd + r]), qn_ref[...]).astype(BF16)
        ckv = _rms(_dot(xn, win_ref[:, 3 * cd + r:3 * cd + 2 * r]), kvn_ref[...]).astype(BF16)
        kr = _dot(xn, win_ref[:, 3 * cd + 2 * r:])
        return cq, ckv, kr

    head_lane = lax.broadcasted_iota(jnp.int32, (1, MLA_HEADS * HEAD_PAD), 1) % HEAD_PAD
    v_one = jnp.where(head_lane == MLA_V, 1.0, 0.0)

    def up(rs, cq, ckv, kr):
        c, s = c_ref[rs, :], s_ref[rs, :]
        q_ref[rs, :] = (_rope(_dot(cq, wuq_ref[...]), c, s) * q_scale).astype(BF16)
        k_rope = _rope(kr, c, s)
        k_ref[rs, :] = (_dot(ckv, wuk_ref[...])
                        + jnp.concatenate([k_rope] * MLA_HEADS, axis=1)).astype(BF16)
        v_ref[rs, :] = (_dot(ckv, wuv_ref[...]) + v_one).astype(BF16)

    low = None
    for c, (rs, xn) in enumerate(zip(chains, xns)):
        nxt = down(rs, xn)
        if low is not None:
            up(chains[c - 1], *low)
        low = nxt
    up(chains[-1], *low)


def _hyb_in(x, g, win, convw, qn, kvn, wuq, wuk, wuv, layer, c_tab, s_tab, casts, seq, tm,
            n_chains):
    t = x.shape[0]
    hp = MLA_HEADS * HEAD_PAD
    row = lambda n: pl.BlockSpec((tm, n), lambda i: (i, 0))
    full = _resident
    stacked = lambda a: _resident(a, layer)
    plans = [_cast_plan(w, l, axis, t // tm) for w, l, axis in casts]
    return pl.pallas_call(
        functools.partial(_hyb_in_kernel, tiles_per_seq=seq // tm, n_chains=n_chains,
                          q_scale=ATTN_SCALE * LOG2_E),
        grid=(t // tm,),
        in_specs=[row(D_MODEL), full(g), stacked(win), full(convw), full(qn), full(kvn),
                  stacked(wuq), stacked(wuk), stacked(wuv), row(LANES), row(LANES)]
        + [p[0] for p in plans],
        out_specs=[row(CONV_DIM), row(hp), row(hp), row(hp)] + [p[1] for p in plans],
        out_shape=[jax.ShapeDtypeStruct((t, CONV_DIM), BF16),
                   jax.ShapeDtypeStruct((t, hp), BF16),
                   jax.ShapeDtypeStruct((t, hp), BF16),
                   jax.ShapeDtypeStruct((t, hp), BF16)] + [p[2] for p in plans],
        scratch_shapes=[pltpu.VMEM((tm + 8, CONV_DIM), F32)],
        compiler_params=_params("arbitrary"),
        name="hyb_in",
    )(x, g, win, convw, qn, kvn, wuq, wuk, wuv, c_tab, s_tab, *[w for w, _, _ in casts])


def _attn_kernel(q_ref, k_ref, v_ref, o_ref, *, blk):
    seq = q_ref.shape[0]
    lane = lax.broadcasted_iota(jnp.int32, (blk, 2 * MLA_V), 1)
    first = lane < MLA_V
    rows = lax.broadcasted_iota(jnp.int32, (blk, blk), 0) // CHUNK
    cols = lax.broadcasted_iota(jnp.int32, (blk, blk), 1) // CHUNK
    visible = cols <= rows

    def scores(i, h):
        q0 = i * blk
        hs = slice(h * HEAD_PAD, (h + 1) * HEAD_PAD)
        s = _dot_nt(q_ref[q0:q0 + blk, hs], k_ref[0:q0 + blk, hs])
        s_diag = jnp.where(visible, s[:, q0:], -jnp.inf)
        return (jnp.concatenate([s[:, :q0], s_diag], axis=1) if i > 0 else s_diag,)

    def attend(i, h, s):
        p = jnp.exp2(s - jnp.max(s, axis=1, keepdims=True)).astype(BF16)
        pv = _dot(p, v_ref[0:(i + 1) * blk, h * HEAD_PAD:(h + 1) * HEAD_PAD])
        return pv * (1.0 / pv[:, MLA_V:MLA_V + 1])

    units = [(i, h) for i in range(seq // blk) for h in range(2)]
    pending = scores(*units[0])
    outs = []
    for u, (i, h) in enumerate(units):
        ahead = scores(*units[u + 1]) if u + 1 < len(units) else None
        outs.append(attend(i, h, *pending))
        pending = ahead
        if h == 1:
            both = jnp.where(first, outs[0], pltpu.roll(outs[1], MLA_V, 1))
            o_ref[i * blk:(i + 1) * blk, :] = both.astype(o_ref.dtype)
            outs = []


def _attention(q, k, v, batch, seq, blk):
    t = q.shape[0]
    pairs = MLA_HEADS // 2
    spec = lambda n: pl.BlockSpec((seq, n), lambda b, p: (b, p))
    return pl.pallas_call(
        functools.partial(_attn_kernel, blk=blk),
        grid=(batch, pairs),
        in_specs=[spec(2 * HEAD_PAD), spec(2 * HEAD_PAD), spec(2 * HEAD_PAD)],
        out_specs=spec(2 * MLA_V),
        out_shape=jax.ShapeDtypeStruct((t, MLA_HEADS * MLA_V), BF16),
        compiler_params=_params("parallel", "parallel"),
        name="mla_attention",
    )(q, k, v)


def _ml_in_kernel(x_ref, g_ref, win_ref, bias_ref, *rest, n_chains):
    n_cast = (len(rest) - 7) // 2
    q_ref, k_ref, vt_ref, o_ref, gate_ref = rest[n_cast:n_cast + 5]
    wvt_sc, wgate_sc = rest[-2:]
    _cast_slices(rest[:n_cast] + rest[n_cast + 5:-2])
    rc = x_ref.shape[0] // n_chains
    d = D_MODEL

    @pl.when(pl.program_id(0) == 0)
    def _():
        for j in range(d // ML_DH):
            cs = slice(j * ML_DH, (j + 1) * ML_DH)
            wvt_sc[cs, :] = win_ref[:, 2 * d + cs.start:2 * d + cs.stop].astype(F32).T.astype(BF16)
        wgate_sc[...] = jnp.zeros(wgate_sc.shape, BF16)
        wgate_sc[:, 0:2 * ML_HEADS] = win_ref[:, 4 * d:4 * d + 2 * ML_HEADS]

    chains = [slice(c * rc, (c + 1) * rc) for c in range(n_chains)]
    xns = [_rms(x_ref[rs, :], g_ref[...]).astype(BF16) for rs in chains]
    for rs, xn in zip(chains, xns):
        q_ref[rs, :] = (_dot(xn, win_ref[:, 0:d]) * (ML_DH ** -0.5)).astype(BF16)
        k_ref[rs, :] = _dot(xn, win_ref[:, d:2 * d]).astype(BF16)
        vt_ref[:, rs] = _dot_nt(wvt_sc[...], xn).astype(BF16)
        o_ref[rs, :] = jax.nn.sigmoid(_dot(xn, win_ref[:, 3 * d:4 * d])).astype(BF16)
        gate_ref[rs, :] = _dot(xn, wgate_sc[...]) + bias_ref[...]


def _ml_in(x, g, win, bias, casts, batch, seq, tm, n_chains):
    t = x.shape[0]
    tps = seq // tm
    row = lambda n: pl.BlockSpec((tm, n), lambda i: (i, 0))
    act = jax.ShapeDtypeStruct((t, D_MODEL), BF16)
    plans = [_cast_plan(w, l, axis, t // tm) for w, l, axis in casts]
    return pl.pallas_call(
        functools.partial(_ml_in_kernel, n_chains=n_chains),
        grid=(t // tm,),
        in_specs=[row(D_MODEL), _resident(g), _resident(win), _resident(bias)]
        + [p[0] for p in plans],
        out_specs=[row(D_MODEL), row(D_MODEL),
                   pl.BlockSpec((None, D_MODEL, tm), lambda i: (i // tps, 0, i % tps)),
                   row(D_MODEL), row(LANES)] + [p[1] for p in plans],
        out_shape=[act, act, jax.ShapeDtypeStruct((batch, D_MODEL, seq), BF16), act,
                   jax.ShapeDtypeStruct((t, LANES), F32)] + [p[2] for p in plans],
        scratch_shapes=[pltpu.VMEM((D_MODEL, D_MODEL), BF16), pltpu.VMEM((D_MODEL, LANES), BF16)],
        compiler_params=_params("arbitrary"),
        name="mlstm_in",
    )(x, g, win, bias, *[w for w, _, _ in casts])


def _log_sigmoid(x):
    return jnp.minimum(x, 0.0) - jnp.log1p(jnp.exp(-jnp.abs(x)))


def _ml_cell_kernel(q_ref, k_ref, vt_ref, gate_ref, out_ref, st_sc, m_sc):
    nseq, ln = q_ref.shape[0], q_ref.shape[1]

    @pl.when(pl.program_id(1) == 0)
    def _():
        st_sc[...] = jnp.zeros(st_sc.shape, F32)
        m_sc[...] = jnp.zeros(m_sc.shape, F32)

    units = [(s, h) for s in range(nseq) for h in range(ML_HEADS)]
    ids = range(len(units))
    cols = lambda h: slice(h * ML_DH, (h + 1) * ML_DH)
    q = [q_ref[s, :, cols(h)] for s, h in units]
    k = [k_ref[s, :, cols(h)] for s, h in units]
    vt = [vt_ref[s, cols(h), :] for s, h in units]
    st_prev = [st_sc[u] for u in ids]
    kq = [_dot_nt(k[u], q[u]) for u in ids]
    carried = [_dot_nt(st_prev[u].astype(BF16), q[u]) for u in ids]

    r_idx = lax.broadcasted_iota(jnp.int32, (ln, ln), 0)
    c_idx = lax.broadcasted_iota(jnp.int32, (ln, ln), 1)
    visible = r_idx <= c_idx
    tril = jnp.where(c_idx <= r_idx, 1.0, 0.0).astype(BF16)
    first_row = lax.broadcasted_iota(jnp.int32, (ML_EXT, ln), 0) == 0
    gates, gates_t, bcum, bcum_t = [], [], [], []
    for s in range(nseq):
        g = gate_ref[s]
        lsig = _log_sigmoid(g)
        hi = lsig.astype(BF16)
        rest = lsig - hi.astype(F32)
        mid = rest.astype(BF16)
        low = (rest - mid.astype(F32)).astype(BF16)
        b = _dot(tril, hi) + _dot(tril, mid) + _dot(tril, low)
        gates.append(g)
        gates_t.append(g.T)
        bcum.append(b)
        bcum_t.append(b.T)

    scores, vtws, inters, recips, stats = [], [], [], [], []
    for u, (s, h) in enumerate(units):
        li_row = gates_t[s][h:h + 1, :]
        b_row = bcum_t[s][ML_HEADS + h:ML_HEADS + h + 1, :]
        bl_col = bcum[s][:, ML_HEADS + h:ML_HEADS + h + 1] - gates[s][:, h:h + 1]
        d_log = jnp.where(visible, b_row - bl_col, -jnp.inf)
        m_prev = m_sc[u][:, 0:1]
        m_t = jnp.maximum(b_row + m_prev, jnp.max(d_log, axis=0, keepdims=True))
        inter = jnp.exp(b_row + m_prev - m_t)
        sc = kq[u] * jnp.exp(d_log - m_t)
        den = inter * carried[u][ML_DH:ML_DH + 1, :] + jnp.sum(sc, axis=0, keepdims=True)
        recips.append(1.0 / jnp.maximum(jnp.abs(den), jnp.exp(-m_t)))
        scores.append(sc.astype(BF16))
        inters.append(inter)
        b_last = b_row[:, ln - 1:ln]
        a_row = b_last - b_row + li_row
        a_max = jnp.max(a_row, axis=1, keepdims=True)
        e_row = jnp.exp(a_row - a_max)
        vtws.append(jnp.concatenate([vt[u].astype(F32) * e_row, jnp.where(first_row, e_row, 0.0)],
                                    axis=0).astype(BF16))
        m_new = jnp.maximum(b_last + m_prev, a_max)
        stats.append((jnp.exp(b_last + m_prev - m_new), jnp.exp(a_max - m_new)))
        m_sc[u] = jnp.broadcast_to(m_new, (1, LANES))

    sv = [_dot(vt[u], scores[u]) for u in ids]
    local = [_dot(vtws[u], k[u]) for u in ids]

    for u, (s, h) in enumerate(units):
        num = inters[u] * carried[u][:ML_DH, :] + sv[u]
        out_ref[s, cols(h), :] = (num * recips[u]).astype(out_ref.dtype)
        sp, sn = stats[u]
        st_sc[u] = sp * st_prev[u] + sn * local[u]


def _ml_cell(q, k, vt, gates, batch, seq, ln, nseq):
    nc = seq // ln
    units = nseq * ML_HEADS
    view = lambda a: a.reshape(batch, seq, a.shape[-1])
    blk = lambda n: pl.BlockSpec((nseq, ln, n), lambda b, c: (b, c, 0))
    blk_t = pl.BlockSpec((nseq, D_MODEL, ln), lambda b, c: (b, 0, c))
    return pl.pallas_call(
        _ml_cell_kernel,
        grid=(batch // nseq, nc),
        in_specs=[blk(D_MODEL), blk(D_MODEL), blk_t, blk(LANES)],
        out_specs=blk_t,
        out_shape=jax.ShapeDtypeStruct((batch, D_MODEL, seq), BF16),
        scratch_shapes=[pltpu.VMEM((units, ML_DH + ML_EXT, ML_DH), F32),
                        pltpu.VMEM((units, 1, LANES), F32)],
        compiler_params=_params("parallel", "arbitrary"),
        name="mlstm_cell",
    )(view(q), view(k), vt, view(gates))


def _out_mlp_kernel(*refs, n_mix, gated, n_chains):
    mix = refs[:n_mix]
    wout_ref, x_ref, gpost_ref, gpre_ref, gmlp_ref, w1_ref, w2_ref, out_ref = refs[n_mix:]
    rc = x_ref.shape[0] // n_chains

    def mixed(rs):
        if not gated:
            return jnp.concatenate([p[rs, :] for p in mix], axis=1)
        cell_ref, gate_ref, hnorm_ref = mix
        heads = []
        for h in range(ML_HEADS):
            cs = slice(h * ML_DH, (h + 1) * ML_DH)
            ct = cell_ref[cs, rs].astype(F32)
            scale = lax.rsqrt(jnp.mean(ct * ct, axis=0, keepdims=True) + EPS)
            normed = (ct * scale).T * hnorm_ref[:, cs]
            heads.append((gate_ref[rs, cs].astype(F32) * normed).astype(BF16))
        return jnp.concatenate(heads, axis=1)

    x1s = []
    for c in range(n_chains):
        rs = slice(c * rc, (c + 1) * rc)
        y = _dot(mixed(rs), wout_ref[...])
        x1s.append(x_ref[rs, :] + _rms(y, gpost_ref[...]))
    for c in range(n_chains):
        rs = slice(c * rc, (c + 1) * rc)
        hn = _rms(x1s[c], gpre_ref[...]).astype(BF16)
        a = jnp.square(jnp.maximum(_dot(hn, w1_ref[...]), 0.0)).astype(BF16)
        out_ref[rs, :] = x1s[c] + _rms(_dot(a, w2_ref[...]), gmlp_ref[...])


def _out_mlp(mix, gated, wout, x, gpost, gpre, gmlp, w1, w2, tm, n_chains):
    t = x.shape[0]
    row = lambda n: pl.BlockSpec((tm, n), lambda i: (i, 0))
    vec = _resident(gpost)
    if gated:
        tps = mix[0].shape[2] // tm
        mix_specs = [pl.BlockSpec((None, D_MODEL, tm), lambda i: (i // tps, 0, i % tps)),
                     row(D_MODEL), vec]
    else:
        mix_specs = [row(p.shape[1]) for p in mix]
    return pl.pallas_call(
        functools.partial(_out_mlp_kernel, n_mix=len(mix), gated=gated, n_chains=n_chains),
        grid=(t // tm,),
        in_specs=mix_specs
        + [_resident(wout), row(D_MODEL), vec, vec, vec, _resident(w1), _resident(w2)],
        out_specs=row(D_MODEL),
        out_shape=jax.ShapeDtypeStruct((t, D_MODEL), F32),
        compiler_params=_params("parallel"),
        name="out_mlp",
    )(*mix, wout, x, gpost, gpre, gmlp, w1, w2)


def _hyb_weights(w_in, w_uq, w_ukv):
    n = w_in.shape[0]
    split = 3 * CONV_DIM + 2 * MLA_RANK
    src = _head_lanes()
    heads = lambda w: w.astype(BF16).reshape(n, MLA_RANK, MLA_HEADS * HEAD_PAD)
    k_r = _place(w_in[:, :, split:], np.where(src >= MLA_NOPE, src - MLA_NOPE, -1))
    win = jnp.concatenate([w_in[:, :, :split], k_r], axis=2).astype(BF16)
    wuq = heads(_place(w_uq.reshape(n, MLA_RANK, MLA_HEADS, MLA_NOPE + MLA_ROPE), src))
    ukv = w_ukv.reshape(n, MLA_RANK, MLA_HEADS, MLA_NOPE + MLA_V)
    wuk = heads(_place(ukv[..., :MLA_NOPE], src))
    wuv = heads(_place(ukv[..., MLA_NOPE:], np.where(np.arange(HEAD_PAD) < MLA_V, np.arange(HEAD_PAD), -1)))
    return win, wuq, wuk, wuv


def kernel(x, positions, norm_mix_pre, norm_mix_post, norm_mlp_pre, norm_mlp_post, hyb_w_in, conv_w, mla_q_norm, mla_kv_norm, mla_w_uq, mla_w_ukv, hyb_w_out, ml_w_in, ml_b_i, ml_b_f, ml_head_norm, ml_w_out, mlp_w1, mlp_w2):
    batch, seq, d = x.shape
    t = batch * seq
    depth = norm_mix_pre.shape[0]
    tm_in = min(1024, seq)
    in_chains = 4
    tm_mlp = min(1024, seq)
    mlp_chains = 4
    blk = min(256, seq)
    cell_seqs = 2 if batch % 2 == 0 else 1

    xt = x.reshape(t, d)
    c_tab, s_tab = _rope_tables(positions)
    vec = lambda a: a.reshape(1, -1)

    hyb_win, hyb_wuq, hyb_wuk, hyb_wuv = _hyb_weights(hyb_w_in, mla_w_uq, mla_w_ukv)
    ml_win = None

    for l in range(depth):
        e = l // 2
        casts = [(hyb_w_out if l % 2 == 0 else ml_w_out, e, 0), (mlp_w1, l, 1), (mlp_w2, l, 0)]
        if l % 2 == 0:
            if l + 1 < depth:
                casts.append((ml_w_in, e, 0))
            y_a, q, k, v, wout, w1, w2, *nxt = _hyb_in(
                xt, vec(norm_mix_pre[l]), hyb_win, conv_w[e], vec(mla_q_norm[e]),
                vec(mla_kv_norm[e]), hyb_wuq, hyb_wuk, hyb_wuv, e, c_tab, s_tab, casts,
                seq, tm_in, in_chains)
            ml_win = nxt[0] if nxt else None
            y_b = _attention(q, k, v, batch, seq, blk)
            mix = [y_a, y_b]
        else:
            bias = jnp.pad(jnp.concatenate([ml_b_i[e], ml_b_f[e]]), (0, LANES - 2 * ML_HEADS))[None, :]
            q, k, vt, o_gate, gates, wout, w1, w2 = _ml_in(
                xt, vec(norm_mix_pre[l]), ml_win, bias, casts, batch, seq, tm_in, in_chains)
            cell = _ml_cell(q, k, vt, gates, batch, seq, blk, cell_seqs)
            mix = [cell, o_gate, vec(ml_head_norm[e])]
        xt = _out_mlp(mix, l % 2 == 1, wout, xt, vec(norm_mix_post[l]), vec(norm_mlp_pre[l]),
                      vec(norm_mlp_post[l]), w1, w2, tm_mlp, mlp_chains)
    return xt.reshape(batch, seq, d)
```

```python
import functools

import jax
import jax.numpy as jnp
import numpy as np
from jax import lax
from jax.experimental import pallas as pl
from jax.experimental.pallas import tpu as pltpu

D_MODEL = 1024
EPS = 1e-6
CHUNK = 64
CONV_DIM = 512
CONV_WIDTH = 3
MLA_HEADS = 8
MLA_RANK = 256
MLA_NOPE = 64
MLA_ROPE = 32
MLA_V = 64
ROPE_THETA = 10000.0
ML_HEADS = 4
ML_DH = 256
D_FF = 4096

ATTN_SCALE = (MLA_NOPE + MLA_ROPE) ** -0.5
LOG2_E = 1.4426950408889634

LANES = 128
HEAD_PAD = LANES
ML_EXT = 16
VMEM_LIMIT = 56 * 1024 * 1024

BF16 = jnp.bfloat16
F32 = jnp.float32


def _dot(a, b):
    return jnp.dot(a, b, preferred_element_type=F32)


def _dot_nt(a, b):
    return lax.dot_general(a, b, (((1,), (1,)), ((), ())), preferred_element_type=F32)


def _rms(x, g):
    return x * lax.rsqrt(jnp.mean(x * x, axis=-1, keepdims=True) + EPS) * g


def _params(*semantics):
    return pltpu.CompilerParams(dimension_semantics=semantics, vmem_limit_bytes=VMEM_LIMIT)


def _resident(a, layer=None):
    if layer is None:
        return pl.BlockSpec(a.shape, lambda *_: (0,) * a.ndim, pipeline_mode=pl.Buffered(1))
    return pl.BlockSpec((None,) + a.shape[1:], lambda *_: (layer,) + (0,) * (a.ndim - 1),
                        pipeline_mode=pl.Buffered(1))


def _cast_plan(w, layer, axis, steps):
    r, c = w.shape[1:]
    if axis == 0:
        blk, imap_in, imap_out = (r // steps, c), (lambda i: (layer, i, 0)), (lambda i: (i, 0))
    else:
        blk, imap_in, imap_out = (r, c // steps), (lambda i: (layer, 0, i)), (lambda i: (0, i))
    return (pl.BlockSpec((None,) + blk, imap_in), pl.BlockSpec(blk, imap_out),
            jax.ShapeDtypeStruct((r, c), BF16))


def _cast_slices(refs):
    n = len(refs) // 2
    for src, dst in zip(refs[:n], refs[n:]):
        dst[...] = src[...].astype(BF16)


def _rope_table_kernel(pos_ref, inv_ref, cos_ref, sin_ref):
    ang = pos_ref[...].astype(F32) * inv_ref[...]
    cos_ref[...] = jnp.cos(ang)
    sin_ref[...] = jnp.sin(ang)


def _head_lanes():
    half = MLA_ROPE // 2
    x1 = LANES // 2 - half
    src = -np.ones((LANES,), np.int32)
    src[:x1] = np.arange(x1)
    src[x1:x1 + half] = MLA_NOPE + np.arange(half)
    src[LANES // 2:LANES // 2 + MLA_NOPE - x1] = x1 + np.arange(MLA_NOPE - x1)
    src[LANES - half:] = MLA_NOPE + half + np.arange(half)
    return src


def _place(w, src):
    n = w.shape[-1]
    idx = np.where((src >= 0) & (src < n), src, n)
    return jnp.take(jnp.pad(w, [(0, 0)] * (w.ndim - 1) + [(0, 1)]), idx, axis=-1)


def _rope_tables(positions):
    t = positions.size
    half = MLA_ROPE // 2
    per_row = LANES // half
    inv = ROPE_THETA ** (-jnp.arange(half, dtype=F32) / half)
    pos = jnp.repeat(positions.reshape(t // per_row, per_row), half, axis=1)
    cos, sin = pl.pallas_call(
        _rope_table_kernel,
        out_shape=[jax.ShapeDtypeStruct((t // per_row, LANES), F32)] * 2,
        name="rope_tables",
    )(pos, jnp.tile(inv, per_row)[None, :])
    cos, sin = cos.reshape(t, half), sin.reshape(t, half)
    rope_src = _head_lanes() - MLA_NOPE
    on_rope = jnp.asarray(rope_src >= 0)
    c_tab = jnp.where(on_rope, _place(jnp.concatenate([cos, cos], axis=1), rope_src), 1.0)
    s_tab = _place(jnp.concatenate([-sin, sin], axis=1), rope_src)
    return c_tab, s_tab


def _rope(v, c, s):
    out = []
    for h in range(v.shape[1] // HEAD_PAD):
        vb = v[:, h * HEAD_PAD:(h + 1) * HEAD_PAD]
        out.append(vb * c + pltpu.roll(vb, HEAD_PAD // 2, 1) * s)
    return out[0] if len(out) == 1 else jnp.concatenate(out, axis=1)


def _hyb_in_kernel(x_ref, g_ref, win_ref, convw_ref, qn_ref, kvn_ref, wuq_ref, wuk_ref,
                   wuv_ref, c_ref, s_ref, *rest, tiles_per_seq, n_chains, q_scale):
    n_cast = (len(rest) - 5) // 2
    ya_ref, q_ref, k_ref, v_ref = rest[n_cast:n_cast + 4]
    u_sc = rest[-1]
    _cast_slices(rest[:n_cast] + rest[n_cast + 4:-1])
    tm = x_ref.shape[0]
    rc = tm // n_chains
    cd, r = CONV_DIM, MLA_RANK

    @pl.when(pl.program_id(0) % tiles_per_seq == 0)
    def _():
        u_sc[0:8, :] = jnp.zeros((8, cd), F32)

    @pl.when(pl.program_id(0) % tiles_per_seq != 0)
    def _():
        u_sc[0:8, :] = u_sc[tm:tm + 8, :]

    chains = [slice(c * rc, (c + 1) * rc) for c in range(n_chains)]
    xns = [_rms(x_ref[rs, :], g_ref[...]).astype(BF16) for rs in chains]
    w = convw_ref[...]
    def down(rs, xn):
        b_gate = _dot(xn, win_ref[:, 0:cd])
        u = _dot(xn, win_ref[:, cd:2 * cd]) * _dot(xn, win_ref[:, 2 * cd:3 * cd])
        u_sc[8 + rs.start:8 + rs.stop, :] = u
        y = (w[2:3, :] * u + w[1:2, :] * u_sc[7 + rs.start:7 + rs.stop, :]
             + w[0:1, :] * u_sc[6 + rs.start:6 + rs.stop, :])
        ya_ref[rs, :] = (b_gate * y).astype(BF16)
        cq = _rms(_dot(xn, win_ref[:, 3 * cd:3 * cd + r]), qn_ref[...]).astype(BF16)
        ckv = _rms(_dot(xn, win_ref[:, 3 * cd + r:3 * cd + 2 * r]), kvn_ref[...]).astype(BF16)
        kr = _dot(xn, win_ref[:, 3 * cd + 2 * r:])
        return cq, ckv, kr

    head_lane = lax.broadcasted_iota(jnp.int32, (1, MLA_HEADS * HEAD_PAD), 1) % HEAD_PAD
    v_one = jnp.where(head_lane == MLA_V, 1.0, 0.0)

    def up(rs, cq, ckv, kr):
        c, s = c_ref[rs, :], s_ref[rs, :]
        q_ref[rs, :] = (_rope(_dot(cq, wuq_ref[...]), c, s) * q_scale).astype(BF16)
        k_rope = _rope(kr, c, s)
        k_ref[rs, :] = (_dot(ckv, wuk_ref[...])
                        + jnp.concatenate([k_rope] * MLA_HEADS, axis=1)).astype(BF16)
        v_ref[rs, :] = (_dot(ckv, wuv_ref[...]) + v_one).astype(BF16)

    low = None
    for c, (rs, xn) in enumerate(zip(chains, xns)):
        nxt = down(rs, xn)
        if low is not None:
            up(chains[c - 1], *low)
        low = nxt
    up(chains[-1], *low)


def _hyb_in(x, g, win, convw, qn, kvn, wuq, wuk, wuv, layer, c_tab, s_tab, casts, seq, tm,
            n_chains):
    t = x.shape[0]
    hp = MLA_HEADS * HEAD_PAD
    row = lambda n: pl.BlockSpec((tm, n), lambda i: (i, 0))
    full = _resident
    stacked = lambda a: _resident(a, layer)
    plans = [_cast_plan(w, l, axis, t // tm) for w, l, axis in casts]
    return pl.pallas_call(
        functools.partial(_hyb_in_kernel, tiles_per_seq=seq // tm, n_chains=n_chains,
                          q_scale=ATTN_SCALE * LOG2_E),
        grid=(t // tm,),
        in_specs=[row(D_MODEL), full(g), stacked(win), full(convw), full(qn), full(kvn),
                  stacked(wuq), stacked(wuk), stacked(wuv), row(LANES), row(LANES)]
        + [p[0] for p in plans],
        out_specs=[row(CONV_DIM), row(hp), row(hp), row(hp)] + [p[1] for p in plans],
        out_shape=[jax.ShapeDtypeStruct((t, CONV_DIM), BF16),
                   jax.ShapeDtypeStruct((t, hp), BF16),
                   jax.ShapeDtypeStruct((t, hp), BF16),
                   jax.ShapeDtypeStruct((t, hp), BF16)] + [p[2] for p in plans],
        scratch_shapes=[pltpu.VMEM((tm + 8, CONV_DIM), F32)],
        compiler_params=_params("arbitrary"),
        name="hyb_in",
    )(x, g, win, convw, qn, kvn, wuq, wuk, wuv, c_tab, s_tab, *[w for w, _, _ in casts])


def _attn_kernel(q_ref, k_ref, v_ref, o_ref, *, blk):
    seq = q_ref.shape[0]
    lane = lax.broadcasted_iota(jnp.int32, (blk, 2 * MLA_V), 1)
    first = lane < MLA_V
    rows = lax.broadcasted_iota(jnp.int32, (blk, blk), 0) // CHUNK
    cols = lax.broadcasted_iota(jnp.int32, (blk, blk), 1) // CHUNK
    visible = cols <= rows

    def scores(i, h):
        q0 = i * blk
        hs = slice(h * HEAD_PAD, (h + 1) * HEAD_PAD)
        s = _dot_nt(q_ref[q0:q0 + blk, hs], k_ref[0:q0 + blk, hs])
        s_diag = jnp.where(visible, s[:, q0:], -jnp.inf)
        return (jnp.concatenate([s[:, :q0], s_diag], axis=1) if i > 0 else s_diag,)

    def attend(i, h, s):
        p = jnp.exp2(s - jnp.max(s, axis=1, keepdims=True)).astype(BF16)
        pv = _dot(p, v_ref[0:(i + 1) * blk, h * HEAD_PAD:(h + 1) * HEAD_PAD])
        return pv * (1.0 / pv[:, MLA_V:MLA_V + 1])

    units = [(i, h) for i in range(seq // blk) for h in range(2)]
    pending = scores(*units[0])
    outs = []
    for u, (i, h) in enumerate(units):
        ahead = scores(*units[u + 1]) if u + 1 < len(units) else None
        outs.append(attend(i, h, *pending))
        pending = ahead
        if h == 1:
            both = jnp.where(first, outs[0], pltpu.roll(outs[1], MLA_V, 1))
            o_ref[i * blk:(i + 1) * blk, :] = both.astype(o_ref.dtype)
            outs = []


def _attention(q, k, v, batch, seq, blk):
    t = q.shape[0]
    pairs = MLA_HEADS // 2
    spec = lambda n: pl.BlockSpec((seq, n), lambda b, p: (b, p))
    return pl.pallas_call(
        functools.partial(_attn_kernel, blk=blk),
        grid=(batch, pairs),
        in_specs=[spec(2 * HEAD_PAD), spec(2 * HEAD_PAD), spec(2 * HEAD_PAD)],
        out_specs=spec(2 * MLA_V),
        out_shape=jax.ShapeDtypeStruct((t, MLA_HEADS * MLA_V), BF16),
        compiler_params=_params("parallel", "parallel"),
        name="mla_attention",
    )(q, k, v)


def _ml_in_kernel(x_ref, g_ref, win_ref, bias_ref, q_ref, k_ref, vt_ref, o_ref, gate_ref,
                  wvt_sc, wgate_sc, *, n_chains):
    rc = x_ref.shape[0] // n_chains
    d = D_MODEL

    @pl.when(pl.program_id(0) == 0)
    def _():
        for j in range(d // ML_DH):
            cs = slice(j * ML_DH, (j + 1) * ML_DH)
            wvt_sc[cs, :] = win_ref[:, 2 * d + cs.start:2 * d + cs.stop].astype(F32).T.astype(BF16)
        wgate_sc[...] = jnp.zeros(wgate_sc.shape, BF16)
        wgate_sc[:, 0:2 * ML_HEADS] = win_ref[:, 4 * d:4 * d + 2 * ML_HEADS]

    chains = [slice(c * rc, (c + 1) * rc) for c in range(n_chains)]
    xns = [_rms(x_ref[rs, :], g_ref[...]).astype(BF16) for rs in chains]
    for rs, xn in zip(chains, xns):
        q_ref[rs, :] = (_dot(xn, win_ref[:, 0:d]) * (ML_DH ** -0.5)).astype(BF16)
        k_ref[rs, :] = _dot(xn, win_ref[:, d:2 * d]).astype(BF16)
        vt_ref[:, rs] = _dot_nt(wvt_sc[...], xn).astype(BF16)
        o_ref[rs, :] = jax.nn.sigmoid(_dot(xn, win_ref[:, 3 * d:4 * d])).astype(BF16)
        gate_ref[rs, :] = _dot(xn, wgate_sc[...]) + bias_ref[...]


def _ml_in(x, g, win, bias, batch, seq, tm, n_chains):
    t = x.shape[0]
    tps = seq // tm
    row = lambda n: pl.BlockSpec((tm, n), lambda i: (i, 0))
    act = jax.ShapeDtypeStruct((t, D_MODEL), BF16)
    return pl.pallas_call(
        functools.partial(_ml_in_kernel, n_chains=n_chains),
        grid=(t // tm,),
        in_specs=[row(D_MODEL), _resident(g), _resident(win), _resident(bias)],
        out_specs=[row(D_MODEL), row(D_MODEL),
                   pl.BlockSpec((None, D_MODEL, tm), lambda i: (i // tps, 0, i % tps)),
                   row(D_MODEL), row(LANES)],
        out_shape=[act, act, jax.ShapeDtypeStruct((batch, D_MODEL, seq), BF16), act,
                   jax.ShapeDtypeStruct((t, LANES), F32)],
        scratch_shapes=[pltpu.VMEM((D_MODEL, D_MODEL), BF16), pltpu.VMEM((D_MODEL, LANES), BF16)],
        compiler_params=_params("arbitrary"),
        name="mlstm_in",
    )(x, g, win, bias)


def _log_sigmoid(x):
    return jnp.minimum(x, 0.0) - jnp.log1p(jnp.exp(-jnp.abs(x)))


def _ml_cell_kernel(q_ref, k_ref, vt_ref, gate_ref, out_ref, st_sc, m_sc):
    nseq, ln = q_ref.shape[0], q_ref.shape[1]

    @pl.when(pl.program_id(1) == 0)
    def _():
        st_sc[...] = jnp.zeros(st_sc.shape, F32)
        m_sc[...] = jnp.zeros(m_sc.shape, F32)

    units = [(s, h) for s in range(nseq) for h in range(ML_HEADS)]
    ids = range(len(units))
    cols = lambda h: slice(h * ML_DH, (h + 1) * ML_DH)
    q = [q_ref[s, :, cols(h)] for s, h in units]
    k = [k_ref[s, :, cols(h)] for s, h in units]
    vt = [vt_ref[s, cols(h), :] for s, h in units]
    st_prev = [st_sc[u] for u in ids]

    r_idx = lax.broadcasted_iota(jnp.int32, (ln, ln), 0)
    c_idx = lax.broadcasted_iota(jnp.int32, (ln, ln), 1)
    visible = r_idx <= c_idx
    tril = jnp.where(c_idx <= r_idx, 1.0, 0.0).astype(BF16)
    first_row = lax.broadcasted_iota(jnp.int32, (ML_EXT, ln), 0) == 0
    gates, gates_t, bcum, bcum_t = [], [], [], []
    for s in range(nseq):
        g = gate_ref[s]
        lsig = _log_sigmoid(g)
        hi = lsig.astype(BF16)
        rest = lsig - hi.astype(F32)
        mid = rest.astype(BF16)
        low = (rest - mid.astype(F32)).astype(BF16)
        b = _dot(tril, hi) + _dot(tril, mid) + _dot(tril, low)
        gates.append(g)
        gates_t.append(g.T)
        bcum.append(b)
        bcum_t.append(b.T)

    kq = [_dot_nt(k[u], q[u]) for u in ids]
    carried = [_dot_nt(st_prev[u].astype(BF16), q[u]) for u in ids]

    scores, vtws, inters, recips, stats = [], [], [], [], []
    for u, (s, h) in enumerate(units):
        li_row = gates_t[s][h:h + 1, :]
        b_row = bcum_t[s][ML_HEADS + h:ML_HEADS + h + 1, :]
        bl_col = bcum[s][:, ML_HEADS + h:ML_HEADS + h + 1] - gates[s][:, h:h + 1]
        d_log = jnp.where(visible, b_row - bl_col, -jnp.inf)
        m_prev = m_sc[u][:, 0:1]
        m_t = jnp.maximum(b_row + m_prev, jnp.max(d_log, axis=0, keepdims=True))
        inter = jnp.exp(b_row + m_prev - m_t)
        sc = kq[u] * jnp.exp(d_log - m_t)
        den = inter * carried[u][ML_DH:ML_DH + 1, :] + jnp.sum(sc, axis=0, keepdims=True)
        recips.append(1.0 / jnp.maximum(jnp.abs(den), jnp.exp(-m_t)))
        scores.append(sc.astype(BF16))
        inters.append(inter)
        b_last = b_row[:, ln - 1:ln]
        a_row = b_last - b_row + li_row
        a_max = jnp.max(a_row, axis=1, keepdims=True)
        e_row = jnp.exp(a_row - a_max)
        vtws.append(jnp.concatenate([vt[u] * e_row.astype(BF16),
                                     jnp.where(first_row, e_row, 0.0).astype(BF16)], axis=0))
        m_new = jnp.maximum(b_last + m_prev, a_max)
        stats.append((jnp.exp(b_last + m_prev - m_new), jnp.exp(a_max - m_new)))
        m_sc[u] = jnp.broadcast_to(m_new, (1, LANES))

    sv = [_dot(vt[u], scores[u]) for u in ids]
    local = [_dot(vtws[u], k[u]) for u in ids]

    for u, (s, h) in enumerate(units):
        num = inters[u] * carried[u][:ML_DH, :] + sv[u]
        out_ref[s, cols(h), :] = (num * recips[u]).astype(out_ref.dtype)
        sp, sn = stats[u]
        st_sc[u] = sp * st_prev[u] + sn * local[u]


def _ml_cell(q, k, vt, gates, batch, seq, ln, nseq):
    nc = seq // ln
    units = nseq * ML_HEADS
    view = lambda a: a.reshape(batch, seq, a.shape[-1])
    blk = lambda n: pl.BlockSpec((nseq, ln, n), lambda b, c: (b, c, 0))
    blk_t = pl.BlockSpec((nseq, D_MODEL, ln), lambda b, c: (b, 0, c))
    return pl.pallas_call(
        _ml_cell_kernel,
        grid=(batch // nseq, nc),
        in_specs=[blk(D_MODEL), blk(D_MODEL), blk_t, blk(LANES)],
        out_specs=blk_t,
        out_shape=jax.ShapeDtypeStruct((batch, D_MODEL, seq), BF16),
        scratch_shapes=[pltpu.VMEM((units, ML_DH + ML_EXT, ML_DH), F32),
                        pltpu.VMEM((units, 1, LANES), F32)],
        compiler_params=_params("parallel", "arbitrary"),
        name="mlstm_cell",
    )(view(q), view(k), vt, view(gates))


def _out_mlp_kernel(*refs, n_mix, n_cast, gated, n_chains):
    mix = refs[:n_mix]
    wout_ref, x_ref, gpost_ref, gpre_ref, gmlp_ref, w1_ref, w2_ref = refs[n_mix:n_mix + 7]
    out_ref = refs[n_mix + 7 + n_cast]
    _cast_slices(refs[n_mix + 7:n_mix + 7 + n_cast] + refs[n_mix + 8 + n_cast:])
    rc = x_ref.shape[0] // n_chains

    def mixed(rs):
        if not gated:
            return jnp.concatenate([p[rs, :] for p in mix], axis=1)
        cell_ref, gate_ref, hnorm_ref = mix
        heads = []
        for h in range(ML_HEADS):
            cs = slice(h * ML_DH, (h + 1) * ML_DH)
            ct = cell_ref[cs, rs].astype(F32)
            scale = lax.rsqrt(jnp.mean(ct * ct, axis=0, keepdims=True) + EPS)
            normed = (ct * scale).T * hnorm_ref[:, cs]
            heads.append((gate_ref[rs, cs].astype(F32) * normed).astype(BF16))
        return jnp.concatenate(heads, axis=1)

    x1s = []
    for c in range(n_chains):
        rs = slice(c * rc, (c + 1) * rc)
        y = _dot(mixed(rs), wout_ref[...])
        x1s.append(x_ref[rs, :] + _rms(y, gpost_ref[...]))
    for c in range(n_chains):
        rs = slice(c * rc, (c + 1) * rc)
        hn = _rms(x1s[c], gpre_ref[...]).astype(BF16)
        a = jnp.square(jnp.maximum(_dot(hn, w1_ref[...]), 0.0)).astype(BF16)
        out_ref[rs, :] = x1s[c] + _rms(_dot(a, w2_ref[...]), gmlp_ref[...])


def _out_mlp(mix, gated, wout, x, gpost, gpre, gmlp, w1, w2, casts, tm, n_chains):
    t = x.shape[0]
    row = lambda n: pl.BlockSpec((tm, n), lambda i: (i, 0))
    vec = _resident(gpost)
    plans = [_cast_plan(w, l, axis, t // tm) for w, l, axis in casts]
    if gated:
        tps = mix[0].shape[2] // tm
        mix_specs = [pl.BlockSpec((None, D_MODEL, tm), lambda i: (i // tps, 0, i % tps)),
                     row(D_MODEL), vec]
    else:
        mix_specs = [row(p.shape[1]) for p in mix]
    return pl.pallas_call(
        functools.partial(_out_mlp_kernel, n_mix=len(mix), n_cast=len(casts), gated=gated,
                          n_chains=n_chains),
        grid=(t // tm,),
        in_specs=mix_specs
        + [_resident(wout), row(D_MODEL), vec, vec, vec, _resident(w1), _resident(w2)]
        + [p[0] for p in plans],
        out_specs=[row(D_MODEL)] + [p[1] for p in plans],
        out_shape=[jax.ShapeDtypeStruct((t, D_MODEL), F32)] + [p[2] for p in plans],
        compiler_params=_params("parallel"),
        name="out_mlp",
    )(*mix, wout, x, gpost, gpre, gmlp, w1, w2, *[w for w, _, _ in casts])


def _hyb_weights(w_in, w_uq, w_ukv):
    n = w_in.shape[0]
    split = 3 * CONV_DIM + 2 * MLA_RANK
    src = _head_lanes()
    heads = lambda w: w.astype(BF16).reshape(n, MLA_RANK, MLA_HEADS * HEAD_PAD)
    k_r = _place(w_in[:, :, split:], np.where(src >= MLA_NOPE, src - MLA_NOPE, -1))
    win = jnp.concatenate([w_in[:, :, :split], k_r], axis=2).astype(BF16)
    wuq = heads(_place(w_uq.reshape(n, MLA_RANK, MLA_HEADS, MLA_NOPE + MLA_ROPE), src))
    ukv = w_ukv.reshape(n, MLA_RANK, MLA_HEADS, MLA_NOPE + MLA_V)
    wuk = heads(_place(ukv[..., :MLA_NOPE], src))
    wuv = heads(_place(ukv[..., MLA_NOPE:], np.where(np.arange(HEAD_PAD) < MLA_V, np.arange(HEAD_PAD), -1)))
    return win, wuq, wuk, wuv


def kernel(x, positions, norm_mix_pre, norm_mix_post, norm_mlp_pre, norm_mlp_post, hyb_w_in, conv_w, mla_q_norm, mla_kv_norm, mla_w_uq, mla_w_ukv, hyb_w_out, ml_w_in, ml_b_i, ml_b_f, ml_head_norm, ml_w_out, mlp_w1, mlp_w2):
    batch, seq, d = x.shape
    t = batch * seq
    depth = norm_mix_pre.shape[0]
    tm_in = min(1024, seq)
    in_chains = 4
    tm_mlp = min(1024, seq)
    mlp_chains = 4
    blk = min(256, seq)
    cell_seqs = 2 if batch % 2 == 0 else 1

    xt = x.reshape(t, d)
    c_tab, s_tab = _rope_tables(positions)
    vec = lambda a: a.reshape(1, -1)

    hyb_win, hyb_wuq, hyb_wuk, hyb_wuv = _hyb_weights(hyb_w_in, mla_w_uq, mla_w_ukv)

    def layer_casts(l):
        if l >= depth:
            return []
        e = l // 2
        casts = [(hyb_w_out if l % 2 == 0 else ml_w_out, e, 0), (mlp_w1, l, 1), (mlp_w2, l, 0)]
        return casts + ([(ml_w_in, e, 0)] if l % 2 == 1 else [])

    ready = None
    for l in range(depth):
        e = l // 2
        if l % 2 == 0:
            y_a, q, k, v, *own = _hyb_in(
                xt, vec(norm_mix_pre[l]), hyb_win, conv_w[e], vec(mla_q_norm[e]),
                vec(mla_kv_norm[e]), hyb_wuq, hyb_wuk, hyb_wuv, e, c_tab, s_tab,
                layer_casts(l) if ready is None else [], seq, tm_in, in_chains)
            wout, w1, w2 = own if ready is None else ready
            y_b = _attention(q, k, v, batch, seq, blk)
            mix = [y_a, y_b]
        else:
            wout, w1, w2, ml_win = ready
            bias = jnp.pad(jnp.concatenate([ml_b_i[e], ml_b_f[e]]), (0, LANES - 2 * ML_HEADS))[None, :]
            q, k, vt, o_gate, gates = _ml_in(
                xt, vec(norm_mix_pre[l]), ml_win, bias, batch, seq, tm_in, in_chains)
            cell = _ml_cell(q, k, vt, gates, batch, seq, blk, cell_seqs)
            mix = [cell, o_gate, vec(ml_head_norm[e])]
        xt, *ready = _out_mlp(mix, l % 2 == 1, wout, xt, vec(norm_mix_post[l]),
                              vec(norm_mlp_pre[l]), vec(norm_mlp_post[l]), w1, w2,
                              layer_casts(l + 1), tm_mlp, mlp_chains)
    return xt.reshape(batch, seq, d)
```

```python
import functools

import jax
import jax.numpy as jnp
import numpy as np
from jax import lax
from jax.experimental import pallas as pl
from jax.experimental.pallas import tpu as pltpu

D_MODEL = 1024
EPS = 1e-6
CHUNK = 64
CONV_DIM = 512
CONV_WIDTH = 3
MLA_HEADS = 8
MLA_RANK = 256
MLA_NOPE = 64
MLA_ROPE = 32
MLA_V = 64
ROPE_THETA = 10000.0
ML_HEADS = 4
ML_DH = 256
D_FF = 4096

ATTN_SCALE = (MLA_NOPE + MLA_ROPE) ** -0.5
LOG2_E = 1.4426950408889634

LANES = 128
HEAD_PAD = LANES
ML_EXT = 16
ATTN_LOOKAHEAD = 3
VMEM_LIMIT = 56 * 1024 * 1024

BF16 = jnp.bfloat16
F32 = jnp.float32


def _dot(a, b):
    return jnp.dot(a, b, preferred_element_type=F32)


def _dot_nt(a, b):
    return lax.dot_general(a, b, (((1,), (1,)), ((), ())), preferred_element_type=F32)


def _rms(x, g):
    return x * lax.rsqrt(jnp.mean(x * x, axis=-1, keepdims=True) + EPS) * g


def _params(*semantics):
    return pltpu.CompilerParams(dimension_semantics=semantics, vmem_limit_bytes=VMEM_LIMIT)


def _resident(a, layer=None):
    if layer is None:
        return pl.BlockSpec(a.shape, lambda *_: (0,) * a.ndim, pipeline_mode=pl.Buffered(1))
    return pl.BlockSpec((None,) + a.shape[1:], lambda *_: (layer,) + (0,) * (a.ndim - 1),
                        pipeline_mode=pl.Buffered(1))


def _cast_plan(w, layer, axis, steps):
    r, c = w.shape[1:]
    if axis == 0:
        blk, imap_in, imap_out = (r // steps, c), (lambda i: (layer, i, 0)), (lambda i: (i, 0))
    else:
        blk, imap_in, imap_out = (r, c // steps), (lambda i: (layer, 0, i)), (lambda i: (0, i))
    return (pl.BlockSpec((None,) + blk, imap_in), pl.BlockSpec(blk, imap_out),
            jax.ShapeDtypeStruct((r, c), BF16))


def _cast_slices(refs):
    n = len(refs) // 2
    for src, dst in zip(refs[:n], refs[n:]):
        dst[...] = src[...].astype(BF16)


def _rope_table_kernel(pos_ref, lane_ref, c_ref, s_ref):
    rows = pos_ref.shape[0]
    half = MLA_ROPE // 2
    inv, m1, m2, m0 = (lane_ref[r:r + 1, :] for r in range(4))
    ang = pos_ref[...].astype(F32) * inv
    cos, sin = jnp.cos(ang), jnp.sin(ang)
    for g in range(LANES // half):
        to1 = (LANES // 2 - half - g * half) % LANES
        to2 = (LANES - half - g * half) % LANES
        out = slice(g * rows, (g + 1) * rows)
        c_ref[out, :] = m0 + pltpu.roll(cos, to1, 1) * m1 + pltpu.roll(cos, to2, 1) * m2
        s_ref[out, :] = pltpu.roll(sin, to2, 1) * m2 - pltpu.roll(sin, to1, 1) * m1


def _head_lanes():
    half = MLA_ROPE // 2
    x1 = LANES // 2 - half
    src = -np.ones((LANES,), np.int32)
    src[:x1] = np.arange(x1)
    src[x1:x1 + half] = MLA_NOPE + np.arange(half)
    src[LANES // 2:LANES // 2 + MLA_NOPE - x1] = x1 + np.arange(MLA_NOPE - x1)
    src[LANES - half:] = MLA_NOPE + half + np.arange(half)
    return src


def _place(w, src):
    n = w.shape[-1]
    idx = np.where((src >= 0) & (src < n), src, n)
    return jnp.take(jnp.pad(w, [(0, 0)] * (w.ndim - 1) + [(0, 1)]), idx, axis=-1)


def _rope_tables(positions):
    t = positions.size
    half = MLA_ROPE // 2
    groups = LANES // half
    inv = ROPE_THETA ** (-jnp.arange(half, dtype=F32) / half)
    rope_src = _head_lanes() - MLA_NOPE
    first = ((rope_src >= 0) & (rope_src < half)).astype(np.float32)
    second = (rope_src >= half).astype(np.float32)
    lanes = jnp.stack([jnp.tile(inv, groups), jnp.asarray(first), jnp.asarray(second),
                       jnp.asarray(1.0 - first - second)]
                      + [jnp.zeros((LANES,), F32)] * 4)
    pos = jnp.repeat(positions.reshape(groups, t // groups).T, half, axis=1)
    return pl.pallas_call(
        _rope_table_kernel,
        out_shape=[jax.ShapeDtypeStruct((t, LANES), F32)] * 2,
        compiler_params=pltpu.CompilerParams(vmem_limit_bytes=VMEM_LIMIT),
        name="rope_tables",
    )(pos, lanes)


def _rope(v, c, s):
    out = []
    for h in range(v.shape[1] // HEAD_PAD):
        vb = v[:, h * HEAD_PAD:(h + 1) * HEAD_PAD]
        out.append(vb * c + pltpu.roll(vb, HEAD_PAD // 2, 1) * s)
    return out[0] if len(out) == 1 else jnp.concatenate(out, axis=1)


def _hyb_in_kernel(x_ref, g_ref, win_ref, convw_ref, qn_ref, kvn_ref, wuq_ref, wuk_ref,
                   wuv_ref, c_ref, s_ref, *rest, tiles_per_seq, n_chains, q_scale):
    n_cast = (len(rest) - 5) // 2
    ya_ref, q_ref, k_ref, v_ref = rest[n_cast:n_cast + 4]
    u_sc = rest[-1]
    _cast_slices(rest[:n_cast] + rest[n_cast + 4:-1])
    tm = x_ref.shape[0]
    rc = tm // n_chains
    cd, r = CONV_DIM, MLA_RANK

    @pl.when(pl.program_id(0) % tiles_per_seq == 0)
    def _():
        u_sc[0:8, :] = jnp.zeros((8, cd), F32)

    @pl.when(pl.program_id(0) % tiles_per_seq != 0)
    def _():
        u_sc[0:8, :] = u_sc[tm:tm + 8, :]

    chains = [slice(c * rc, (c + 1) * rc) for c in range(n_chains)]
    xns = [_rms(x_ref[rs, :], g_ref[...]).astype(BF16) for rs in chains]
    w = convw_ref[...]
    def down(rs, xn):
        b_gate = _dot(xn, win_ref[:, 0:cd])
        u = _dot(xn, win_ref[:, cd:2 * cd]) * _dot(xn, win_ref[:, 2 * cd:3 * cd])
        u_sc[8 + rs.start:8 + rs.stop, :] = u
        y = (w[2:3, :] * u + w[1:2, :] * u_sc[7 + rs.start:7 + rs.stop, :]
             + w[0:1, :] * u_sc[6 + rs.start:6 + rs.stop, :])
        ya_ref[rs, :] = (b_gate * y).astype(BF16)
        cq = _rms(_dot(xn, win_ref[:, 3 * cd:3 * cd + r]), qn_ref[...]).astype(BF16)
        ckv = _rms(_dot(xn, win_ref[:, 3 * cd + r:3 * cd + 2 * r]), kvn_ref[...]).astype(BF16)
        kr = _dot(xn, win_ref[:, 3 * cd + 2 * r:])
        return cq, ckv, kr

    head_lane = lax.broadcasted_iota(jnp.int32, (1, MLA_HEADS * HEAD_PAD), 1) % HEAD_PAD
    v_one = jnp.where(head_lane == MLA_V, 1.0, 0.0)

    def up(rs, cq, ckv, kr):
        c, s = c_ref[rs, :], s_ref[rs, :]
        q_ref[rs, :] = (_rope(_dot(cq, wuq_ref[...]), c, s) * q_scale).astype(BF16)
        k_rope = _rope(kr, c, s)
        k_ref[rs, :] = (_dot(ckv, wuk_ref[...])
                        + jnp.concatenate([k_rope] * MLA_HEADS, axis=1)).astype(BF16)
        v_ref[rs, :] = (_dot(ckv, wuv_ref[...]) + v_one).astype(BF16)

    low = None
    for c, (rs, xn) in enumerate(zip(chains, xns)):
        nxt = down(rs, xn)
        if low is not None:
            up(chains[c - 1], *low)
        low = nxt
    up(chains[-1], *low)


def _hyb_in(x, g, win, convw, qn, kvn, wuq, wuk, wuv, layer, c_tab, s_tab, casts, seq, tm,
            n_chains):
    t = x.shape[0]
    hp = MLA_HEADS * HEAD_PAD
    row = lambda n: pl.BlockSpec((tm, n), lambda i: (i, 0))
    full = _resident
    stacked = lambda a: _resident(a, layer)
    plans = [_cast_plan(w, l, axis, t // tm) for w, l, axis in casts]
    return pl.pallas_call(
        functools.partial(_hyb_in_kernel, tiles_per_seq=seq // tm, n_chains=n_chains,
                          q_scale=ATTN_SCALE * LOG2_E),
        grid=(t // tm,),
        in_specs=[row(D_MODEL), full(g), stacked(win), full(convw), full(qn), full(kvn),
                  stacked(wuq), stacked(wuk), stacked(wuv), row(LANES), row(LANES)]
        + [p[0] for p in plans],
        out_specs=[row(CONV_DIM), row(hp), row(hp), row(hp)] + [p[1] for p in plans],
        out_shape=[jax.ShapeDtypeStruct((t, CONV_DIM), BF16),
                   jax.ShapeDtypeStruct((t, hp), BF16),
                   jax.ShapeDtypeStruct((t, hp), BF16),
                   jax.ShapeDtypeStruct((t, hp), BF16)] + [p[2] for p in plans],
        scratch_shapes=[pltpu.VMEM((tm + 8, CONV_DIM), F32)],
        compiler_params=_params("arbitrary"),
        name="hyb_in",
    )(x, g, win, convw, qn, kvn, wuq, wuk, wuv, c_tab, s_tab, *[w for w, _, _ in casts])


def _attn_kernel(q_ref, k_ref, v_ref, o_ref, *, blk):
    seq = q_ref.shape[0]
    lane = lax.broadcasted_iota(jnp.int32, (blk, 2 * MLA_V), 1)
    first = lane < MLA_V
    rows = lax.broadcasted_iota(jnp.int32, (blk, blk), 0) // CHUNK
    cols = lax.broadcasted_iota(jnp.int32, (blk, blk), 1) // CHUNK
    visible = cols <= rows

    def scores(i, h):
        q0 = i * blk
        hs = slice(h * HEAD_PAD, (h + 1) * HEAD_PAD)
        s = _dot_nt(q_ref[q0:q0 + blk, hs], k_ref[0:q0 + blk, hs])
        s_diag = jnp.where(visible, s[:, q0:], -jnp.inf)
        return (jnp.concatenate([s[:, :q0], s_diag], axis=1) if i > 0 else s_diag,)

    def attend(i, h, s):
        p = jnp.exp2(s - jnp.max(s, axis=1, keepdims=True)).astype(BF16)
        pv = _dot(p, v_ref[0:(i + 1) * blk, h * HEAD_PAD:(h + 1) * HEAD_PAD])
        return pv * (1.0 / pv[:, MLA_V:MLA_V + 1])

    units = [(i, h) for i in range(seq // blk) for h in range(2)]
    pending = [scores(*unit) for unit in units[:ATTN_LOOKAHEAD]]
    outs = []
    for u, (i, h) in enumerate(units):
        if u + ATTN_LOOKAHEAD < len(units):
            pending.append(scores(*units[u + ATTN_LOOKAHEAD]))
        outs.append(attend(i, h, *pending.pop(0)))
        if h == 1:
            both = jnp.where(first, outs[0], pltpu.roll(outs[1], MLA_V, 1))
            o_ref[i * blk:(i + 1) * blk, :] = both.astype(o_ref.dtype)
            outs = []


def _attention(q, k, v, batch, seq, blk):
    t = q.shape[0]
    pairs = MLA_HEADS // 2
    spec = lambda n: pl.BlockSpec((seq, n), lambda b, p: (b, p))
    return pl.pallas_call(
        functools.partial(_attn_kernel, blk=blk),
        grid=(batch, pairs),
        in_specs=[spec(2 * HEAD_PAD), spec(2 * HEAD_PAD), spec(2 * HEAD_PAD)],
        out_specs=spec(2 * MLA_V),
        out_shape=jax.ShapeDtypeStruct((t, MLA_HEADS * MLA_V), BF16),
        compiler_params=_params("parallel", "parallel"),
        name="mla_attention",
    )(q, k, v)


def _ml_in_kernel(x_ref, g_ref, win_ref, bias_ref, q_ref, k_ref, vt_ref, o_ref, gate_ref,
                  wvt_sc, wgate_sc, *, n_chains):
    rc = x_ref.shape[0] // n_chains
    d = D_MODEL

    @pl.when(pl.program_id(0) == 0)
    def _():
        for j in range(d // ML_DH):
            cs = slice(j * ML_DH, (j + 1) * ML_DH)
            wvt_sc[cs, :] = win_ref[:, 2 * d + cs.start:2 * d + cs.stop].astype(F32).T.astype(BF16)
        wgate_sc[...] = jnp.zeros(wgate_sc.shape, BF16)
        wgate_sc[:, 0:2 * ML_HEADS] = win_ref[:, 4 * d:4 * d + 2 * ML_HEADS]

    chains = [slice(c * rc, (c + 1) * rc) for c in range(n_chains)]
    xns = [_rms(x_ref[rs, :], g_ref[...]).astype(BF16) for rs in chains]
    for rs, xn in zip(chains, xns):
        q_ref[rs, :] = (_dot(xn, win_ref[:, 0:d]) * (ML_DH ** -0.5)).astype(BF16)
        k_ref[rs, :] = _dot(xn, win_ref[:, d:2 * d]).astype(BF16)
        vt_ref[:, rs] = _dot_nt(wvt_sc[...], xn).astype(BF16)
        o_ref[rs, :] = jax.nn.sigmoid(_dot(xn, win_ref[:, 3 * d:4 * d])).astype(BF16)
        gate_ref[rs, :] = _dot(xn, wgate_sc[...]) + bias_ref[...]


def _ml_in(x, g, win, bias, batch, seq, tm, n_chains):
    t = x.shape[0]
    tps = seq // tm
    row = lambda n: pl.BlockSpec((tm, n), lambda i: (i, 0))
    act = jax.ShapeDtypeStruct((t, D_MODEL), BF16)
    return pl.pallas_call(
        functools.partial(_ml_in_kernel, n_chains=n_chains),
        grid=(t // tm,),
        in_specs=[row(D_MODEL), _resident(g), _resident(win), _resident(bias)],
        out_specs=[row(D_MODEL), row(D_MODEL),
                   pl.BlockSpec((None, D_MODEL, tm), lambda i: (i // tps, 0, i % tps)),
                   row(D_MODEL), row(LANES)],
        out_shape=[act, act, jax.ShapeDtypeStruct((batch, D_MODEL, seq), BF16), act,
                   jax.ShapeDtypeStruct((t, LANES), F32)],
        scratch_shapes=[pltpu.VMEM((D_MODEL, D_MODEL), BF16), pltpu.VMEM((D_MODEL, LANES), BF16)],
        compiler_params=_params("arbitrary"),
        name="mlstm_in",
    )(x, g, win, bias)


def _log_sigmoid(x):
    return jnp.minimum(x, 0.0) - jnp.log1p(jnp.exp(-jnp.abs(x)))


def _ml_cell_kernel(q_ref, k_ref, vt_ref, gate_ref, out_ref, st_sc, m_sc):
    nseq, ln = q_ref.shape[0], q_ref.shape[1]

    @pl.when(pl.program_id(1) == 0)
    def _():
        st_sc[...] = jnp.zeros(st_sc.shape, F32)
        m_sc[...] = jnp.zeros(m_sc.shape, F32)

    units = [(s, h) for s in range(nseq) for h in range(ML_HEADS)]
    ids = range(len(units))
    cols = lambda h: slice(h * ML_DH, (h + 1) * ML_DH)
    q = [q_ref[s, :, cols(h)] for s, h in units]
    k = [k_ref[s, :, cols(h)] for s, h in units]
    vt = [vt_ref[s, cols(h), :] for s, h in units]
    st_prev = [st_sc[u] for u in ids]

    r_idx = lax.broadcasted_iota(jnp.int32, (ln, ln), 0)
    c_idx = lax.broadcasted_iota(jnp.int32, (ln, ln), 1)
    visible = r_idx <= c_idx
    tril = jnp.where(c_idx <= r_idx, 1.0, 0.0).astype(BF16)
    first_row = lax.broadcasted_iota(jnp.int32, (ML_EXT, ln), 0) == 0
    gates, gates_t, bcum, bcum_t = [], [], [], []
    for s in range(nseq):
        g = gate_ref[s]
        lsig = _log_sigmoid(g)
        hi = lsig.astype(BF16)
        rest = lsig - hi.astype(F32)
        mid = rest.astype(BF16)
        low = (rest - mid.astype(F32)).astype(BF16)
        b = _dot(tril, hi) + _dot(tril, mid) + _dot(tril, low)
        gates.append(g)
        gates_t.append(g.T)
        bcum.append(b)
        bcum_t.append(b.T)

    kq = [_dot_nt(k[u], q[u]) for u in ids]
    carried = [_dot_nt(st_prev[u].astype(BF16), q[u]) for u in ids]

    scores, vtws, inters, recips, stats = [], [], [], [], []
    for u, (s, h) in enumerate(units):
        li_row = gates_t[s][h:h + 1, :]
        b_row = bcum_t[s][ML_HEADS + h:ML_HEADS + h + 1, :]
        bl_col = bcum[s][:, ML_HEADS + h:ML_HEADS + h + 1] - gates[s][:, h:h + 1]
        d_log = jnp.where(visible, b_row - bl_col, -jnp.inf)
        m_prev = m_sc[u][:, 0:1]
        m_t = jnp.maximum(b_row + m_prev, jnp.max(d_log, axis=0, keepdims=True))
        inter = jnp.exp(b_row + m_prev - m_t)
        sc = kq[u] * jnp.exp(d_log - m_t)
        den = inter * carried[u][ML_DH:ML_DH + 1, :] + jnp.sum(sc, axis=0, keepdims=True)
        recips.append(1.0 / jnp.maximum(jnp.abs(den), jnp.exp(-m_t)))
        scores.append(sc.astype(BF16))
        inters.append(inter)
        b_last = b_row[:, ln - 1:ln]
        a_row = b_last - b_row + li_row
        a_max = jnp.max(a_row, axis=1, keepdims=True)
        e_row = jnp.exp(a_row - a_max)
        vtws.append(jnp.concatenate([vt[u] * e_row.astype(BF16),
                                     jnp.where(first_row, e_row, 0.0).astype(BF16)], axis=0))
        m_new = jnp.maximum(b_last + m_prev, a_max)
        stats.append((jnp.exp(b_last + m_prev - m_new), jnp.exp(a_max - m_new)))
        m_sc[u] = jnp.broadcast_to(m_new, (1, LANES))

    sv = [_dot(vt[u], scores[u]) for u in ids]
    local = [_dot(vtws[u], k[u]) for u in ids]

    for u, (s, h) in enumerate(units):
        num = inters[u] * carried[u][:ML_DH, :] + sv[u]
        out_ref[s, cols(h), :] = (num * recips[u]).astype(out_ref.dtype)
        sp, sn = stats[u]
        st_sc[u] = sp * st_prev[u] + sn * local[u]


def _ml_cell(q, k, vt, gates, batch, seq, ln, nseq):
    nc = seq // ln
    units = nseq * ML_HEADS
    view = lambda a: a.reshape(batch, seq, a.shape[-1])
    blk = lambda n: pl.BlockSpec((nseq, ln, n), lambda b, c: (b, c, 0))
    blk_t = pl.BlockSpec((nseq, D_MODEL, ln), lambda b, c: (b, 0, c))
    return pl.pallas_call(
        _ml_cell_kernel,
        grid=(batch // nseq, nc),
        in_specs=[blk(D_MODEL), blk(D_MODEL), blk_t, blk(LANES)],
        out_specs=blk_t,
        out_shape=jax.ShapeDtypeStruct((batch, D_MODEL, seq), BF16),
        scratch_shapes=[pltpu.VMEM((units, ML_DH + ML_EXT, ML_DH), F32),
                        pltpu.VMEM((units, 1, LANES), F32)],
        compiler_params=_params("parallel", "arbitrary"),
        name="mlstm_cell",
    )(view(q), view(k), vt, view(gates))


def _out_mlp_kernel(*refs, n_mix, n_cast, gated, chain_rows):
    mix = refs[:n_mix]
    wout_ref, x_ref, gpost_ref, gpre_ref, gmlp_ref, w1_ref, w2_ref = refs[n_mix:n_mix + 7]
    out_ref = refs[n_mix + 7 + n_cast]
    _cast_slices(refs[n_mix + 7:n_mix + 7 + n_cast] + refs[n_mix + 8 + n_cast:])
    starts = np.cumsum((0,) + tuple(chain_rows))
    chains = [slice(int(a), int(b)) for a, b in zip(starts[:-1], starts[1:])]

    def mixed(rs):
        if not gated:
            return jnp.concatenate([p[rs, :] for p in mix], axis=1)
        cell_ref, gate_ref, hnorm_ref = mix
        heads = []
        for h in range(ML_HEADS):
            cs = slice(h * ML_DH, (h + 1) * ML_DH)
            ct = cell_ref[cs, rs].astype(F32)
            scale = lax.rsqrt(jnp.mean(ct * ct, axis=0, keepdims=True) + EPS)
            normed = (ct * scale).T * hnorm_ref[:, cs]
            heads.append((gate_ref[rs, cs].astype(F32) * normed).astype(BF16))
        return jnp.concatenate(heads, axis=1)

    x1s = []
    for rs in chains:
        y = _dot(mixed(rs), wout_ref[...])
        x1s.append(x_ref[rs, :] + _rms(y, gpost_ref[...]))
    for rs, x1 in zip(chains, x1s):
        hn = _rms(x1, gpre_ref[...]).astype(BF16)
        a = jnp.square(jnp.maximum(_dot(hn, w1_ref[...]), 0.0)).astype(BF16)
        out_ref[rs, :] = x1 + _rms(_dot(a, w2_ref[...]), gmlp_ref[...])


def _out_mlp(mix, gated, wout, x, gpost, gpre, gmlp, w1, w2, casts, tm, chain_rows):
    t = x.shape[0]
    row = lambda n: pl.BlockSpec((tm, n), lambda i: (i, 0))
    vec = _resident(gpost)
    plans = [_cast_plan(w, l, axis, t // tm) for w, l, axis in casts]
    if gated:
        tps = mix[0].shape[2] // tm
        mix_specs = [pl.BlockSpec((None, D_MODEL, tm), lambda i: (i // tps, 0, i % tps)),
                     row(D_MODEL), vec]
    else:
        mix_specs = [row(p.shape[1]) for p in mix]
    return pl.pallas_call(
        functools.partial(_out_mlp_kernel, n_mix=len(mix), n_cast=len(casts), gated=gated,
                          chain_rows=chain_rows),
        grid=(t // tm,),
        in_specs=mix_specs
        + [_resident(wout), row(D_MODEL), vec, vec, vec, _resident(w1), _resident(w2)]
        + [p[0] for p in plans],
        out_specs=[row(D_MODEL)] + [p[1] for p in plans],
        out_shape=[jax.ShapeDtypeStruct((t, D_MODEL), F32)] + [p[2] for p in plans],
        compiler_params=_params("parallel"),
        name="out_mlp",
    )(*mix, wout, x, gpost, gpre, gmlp, w1, w2, *[w for w, _, _ in casts])


def _hyb_weights(w_in, w_uq, w_ukv):
    n = w_in.shape[0]
    split = 3 * CONV_DIM + 2 * MLA_RANK
    src = _head_lanes()
    heads = lambda w: w.astype(BF16).reshape(n, MLA_RANK, MLA_HEADS * HEAD_PAD)
    k_r = _place(w_in[:, :, split:], np.where(src >= MLA_NOPE, src - MLA_NOPE, -1))
    win = jnp.concatenate([w_in[:, :, :split], k_r], axis=2).astype(BF16)
    wuq = heads(_place(w_uq.reshape(n, MLA_RANK, MLA_HEADS, MLA_NOPE + MLA_ROPE), src))
    ukv = w_ukv.reshape(n, MLA_RANK, MLA_HEADS, MLA_NOPE + MLA_V)
    wuk = heads(_place(ukv[..., :MLA_NOPE], src))
    wuv = heads(_place(ukv[..., MLA_NOPE:], np.where(np.arange(HEAD_PAD) < MLA_V, np.arange(HEAD_PAD), -1)))
    return win, wuq, wuk, wuv


def kernel(x, positions, norm_mix_pre, norm_mix_post, norm_mlp_pre, norm_mlp_post, hyb_w_in, conv_w, mla_q_norm, mla_kv_norm, mla_w_uq, mla_w_ukv, hyb_w_out, ml_w_in, ml_b_i, ml_b_f, ml_head_norm, ml_w_out, mlp_w1, mlp_w2):
    batch, seq, d = x.shape
    t = batch * seq
    depth = norm_mix_pre.shape[0]
    tm_in = min(1024, seq)
    in_chains = 4
    tm_mlp = min(1024, seq)
    mlp_chains = (tm_mlp // 4,) * 4
    blk = min(256, seq)
    cell_seqs = 2 if batch % 2 == 0 else 1

    xt = x.reshape(t, d)
    c_tab, s_tab = _rope_tables(positions)
    vec = lambda a: a.reshape(1, -1)

    hyb_win, hyb_wuq, hyb_wuk, hyb_wuv = _hyb_weights(hyb_w_in, mla_w_uq, mla_w_ukv)

    def layer_casts(l):
        if l >= depth:
            return []
        e = l // 2
        casts = [(hyb_w_out if l % 2 == 0 else ml_w_out, e, 0), (mlp_w1, l, 1), (mlp_w2, l, 0)]
        return casts + ([(ml_w_in, e, 0)] if l % 2 == 1 else [])

    ready = None
    for l in range(depth):
        e = l // 2
        if l % 2 == 0:
            y_a, q, k, v, *own = _hyb_in(
                xt, vec(norm_mix_pre[l]), hyb_win, conv_w[e], vec(mla_q_norm[e]),
                vec(mla_kv_norm[e]), hyb_wuq, hyb_wuk, hyb_wuv, e, c_tab, s_tab,
                layer_casts(l) if ready is None else [], seq, tm_in, in_chains)
            wout, w1, w2 = own if ready is None else ready
            y_b = _attention(q, k, v, batch, seq, blk)
            mix = [y_a, y_b]
        else:
            wout, w1, w2, ml_win = ready
            bias = jnp.pad(jnp.concatenate([ml_b_i[e], ml_b_f[e]]), (0, LANES - 2 * ML_HEADS))[None, :]
            q, k, vt, o_gate, gates = _ml_in(
                xt, vec(norm_mix_pre[l]), ml_win, bias, batch, seq, tm_in, in_chains)
            cell = _ml_cell(q, k, vt, gates, batch, seq, blk, cell_seqs)
            mix = [cell, o_gate, vec(ml_head_norm[e])]
        xt, *ready = _out_mlp(mix, l % 2 == 1, wout, xt, vec(norm_mix_post[l]),
                              vec(norm_mlp_pre[l]), vec(norm_mlp_post[l]), w1, w2,
                              layer_casts(l + 1), tm_mlp, mlp_chains)
    return xt.reshape(batch, seq, d)
```

```python
import functools

import jax
import jax.numpy as jnp
import numpy as np
from jax import lax
from jax.experimental import pallas as pl
from jax.experimental.pallas import tpu as pltpu

D_MODEL = 1024
EPS = 1e-6
CHUNK = 64
CONV_DIM = 512
CONV_WIDTH = 3
MLA_HEADS = 8
MLA_RANK = 256
MLA_NOPE = 64
MLA_ROPE = 32
MLA_V = 64
ROPE_THETA = 10000.0
ML_HEADS = 4
ML_DH = 256
D_FF = 4096

ATTN_SCALE = (MLA_NOPE + MLA_ROPE) ** -0.5
LOG2_E = 1.4426950408889634

LANES = 128
HEAD_PAD = LANES
ML_EXT = 16
ATTN_LOOKAHEAD = 3
VMEM_LIMIT = 56 * 1024 * 1024

BF16 = jnp.bfloat16
F32 = jnp.float32


def _dot(a, b):
    return jnp.dot(a, b, preferred_element_type=F32)


def _dot_nt(a, b):
    return lax.dot_general(a, b, (((1,), (1,)), ((), ())), preferred_element_type=F32)


def _rms(x, g):
    return x * lax.rsqrt(jnp.mean(x * x, axis=-1, keepdims=True) + EPS) * g


def _params(*semantics):
    return pltpu.CompilerParams(dimension_semantics=semantics, vmem_limit_bytes=VMEM_LIMIT)


def _resident(a, layer=None):
    if layer is None:
        return pl.BlockSpec(a.shape, lambda *_: (0,) * a.ndim, pipeline_mode=pl.Buffered(1))
    return pl.BlockSpec((None,) + a.shape[1:], lambda *_: (layer,) + (0,) * (a.ndim - 1),
                        pipeline_mode=pl.Buffered(1))


def _cast_plan(w, layer, axis, steps):
    r, c = w.shape[1:]
    if axis == 0:
        blk, imap_in, imap_out = (r // steps, c), (lambda i: (layer, i, 0)), (lambda i: (i, 0))
    else:
        blk, imap_in, imap_out = (r, c // steps), (lambda i: (layer, 0, i)), (lambda i: (0, i))
    return (pl.BlockSpec((None,) + blk, imap_in), pl.BlockSpec(blk, imap_out),
            jax.ShapeDtypeStruct((r, c), BF16))


def _cast_slices(refs):
    n = len(refs) // 2
    for src, dst in zip(refs[:n], refs[n:]):
        dst[...] = src[...].astype(BF16)


def _rope_table_kernel(pos_ref, lane_ref, c_ref, s_ref):
    rows = pos_ref.shape[0]
    half = MLA_ROPE // 2
    inv, m1, m2, m0 = (lane_ref[r:r + 1, :] for r in range(4))
    ang = pos_ref[...].astype(F32) * inv
    cos, sin = jnp.cos(ang), jnp.sin(ang)
    for g in range(LANES // half):
        to1 = (LANES // 2 - half - g * half) % LANES
        to2 = (LANES - half - g * half) % LANES
        out = slice(g * rows, (g + 1) * rows)
        c_ref[out, :] = m0 + pltpu.roll(cos, to1, 1) * m1 + pltpu.roll(cos, to2, 1) * m2
        s_ref[out, :] = pltpu.roll(sin, to2, 1) * m2 - pltpu.roll(sin, to1, 1) * m1


def _head_lanes():
    half = MLA_ROPE // 2
    x1 = LANES // 2 - half
    src = -np.ones((LANES,), np.int32)
    src[:x1] = np.arange(x1)
    src[x1:x1 + half] = MLA_NOPE + np.arange(half)
    src[LANES // 2:LANES // 2 + MLA_NOPE - x1] = x1 + np.arange(MLA_NOPE - x1)
    src[LANES - half:] = MLA_NOPE + half + np.arange(half)
    return src


def _place(w, src):
    n = w.shape[-1]
    idx = np.where((src >= 0) & (src < n), src, n)
    return jnp.take(jnp.pad(w, [(0, 0)] * (w.ndim - 1) + [(0, 1)]), idx, axis=-1)


def _rope_tables(positions):
    t = positions.size
    half = MLA_ROPE // 2
    groups = LANES // half
    inv = ROPE_THETA ** (-jnp.arange(half, dtype=F32) / half)
    rope_src = _head_lanes() - MLA_NOPE
    first = ((rope_src >= 0) & (rope_src < half)).astype(np.float32)
    second = (rope_src >= half).astype(np.float32)
    lanes = jnp.stack([jnp.tile(inv, groups), jnp.asarray(first), jnp.asarray(second),
                       jnp.asarray(1.0 - first - second)]
                      + [jnp.zeros((LANES,), F32)] * 4)
    pos = jnp.repeat(positions.reshape(groups, t // groups).T, half, axis=1)
    return pl.pallas_call(
        _rope_table_kernel,
        out_shape=[jax.ShapeDtypeStruct((t, LANES), F32)] * 2,
        compiler_params=pltpu.CompilerParams(vmem_limit_bytes=VMEM_LIMIT),
        name="rope_tables",
    )(pos, lanes)


def _rope(v, c, s):
    out = []
    for h in range(v.shape[1] // HEAD_PAD):
        vb = v[:, h * HEAD_PAD:(h + 1) * HEAD_PAD]
        out.append(vb * c + pltpu.roll(vb, HEAD_PAD // 2, 1) * s)
    return out[0] if len(out) == 1 else jnp.concatenate(out, axis=1)


def _hyb_in_kernel(x_ref, g_ref, win_ref, convw_ref, qn_ref, kvn_ref, wuq_ref, wuk_ref,
                   wuv_ref, c_ref, s_ref, *rest, tiles_per_seq, n_chains, q_scale):
    n_cast = (len(rest) - 5) // 2
    ya_ref, q_ref, k_ref, v_ref = rest[n_cast:n_cast + 4]
    u_sc = rest[-1]
    _cast_slices(rest[:n_cast] + rest[n_cast + 4:-1])
    tm = x_ref.shape[0]
    rc = tm // n_chains
    cd, r = CONV_DIM, MLA_RANK

    @pl.when(pl.program_id(0) % tiles_per_seq == 0)
    def _():
        u_sc[0:8, :] = jnp.zeros((8, cd), F32)

    @pl.when(pl.program_id(0) % tiles_per_seq != 0)
    def _():
        u_sc[0:8, :] = u_sc[tm:tm + 8, :]

    chains = [slice(c * rc, (c + 1) * rc) for c in range(n_chains)]
    xns = [_rms(x_ref[rs, :], g_ref[...]).astype(BF16) for rs in chains]
    w = convw_ref[...]
    def down(rs, xn):
        b_gate = _dot(xn, win_ref[:, 0:cd])
        u = _dot(xn, win_ref[:, cd:2 * cd]) * _dot(xn, win_ref[:, 2 * cd:3 * cd])
        u_sc[8 + rs.start:8 + rs.stop, :] = u
        y = (w[2:3, :] * u + w[1:2, :] * u_sc[7 + rs.start:7 + rs.stop, :]
             + w[0:1, :] * u_sc[6 + rs.start:6 + rs.stop, :])
        ya_ref[rs, :] = (b_gate * y).astype(BF16)
        cq = _rms(_dot(xn, win_ref[:, 3 * cd:3 * cd + r]), qn_ref[...]).astype(BF16)
        ckv = _rms(_dot(xn, win_ref[:, 3 * cd + r:3 * cd + 2 * r]), kvn_ref[...]).astype(BF16)
        kr = _dot(xn, win_ref[:, 3 * cd + 2 * r:])
        return cq, ckv, kr

    head_lane = lax.broadcasted_iota(jnp.int32, (1, MLA_HEADS * HEAD_PAD), 1) % HEAD_PAD
    v_one = jnp.where(head_lane == MLA_V, 1.0, 0.0)

    def up(rs, cq, ckv, kr):
        c, s = c_ref[rs, :], s_ref[rs, :]
        q_ref[rs, :] = (_rope(_dot(cq, wuq_ref[...]), c, s) * q_scale).astype(BF16)
        k_rope = _rope(kr, c, s)
        k_ref[rs, :] = (_dot(ckv, wuk_ref[...])
                        + jnp.concatenate([k_rope] * MLA_HEADS, axis=1)).astype(BF16)
        v_ref[rs, :] = (_dot(ckv, wuv_ref[...]) + v_one).astype(BF16)

    low = None
    for c, (rs, xn) in enumerate(zip(chains, xns)):
        nxt = down(rs, xn)
        if low is not None:
            up(chains[c - 1], *low)
        low = nxt
    up(chains[-1], *low)


def _hyb_in(x, g, win, convw, qn, kvn, wuq, wuk, wuv, layer, c_tab, s_tab, casts, seq, tm,
            n_chains):
    t = x.shape[0]
    hp = MLA_HEADS * HEAD_PAD
    row = lambda n: pl.BlockSpec((tm, n), lambda i: (i, 0))
    full = _resident
    stacked = lambda a: _resident(a, layer)
    plans = [_cast_plan(w, l, axis, t // tm) for w, l, axis in casts]
    return pl.pallas_call(
        functools.partial(_hyb_in_kernel, tiles_per_seq=seq // tm, n_chains=n_chains,
                          q_scale=ATTN_SCALE * LOG2_E),
        grid=(t // tm,),
        in_specs=[row(D_MODEL), full(g), stacked(win), full(convw), full(qn), full(kvn),
                  stacked(wuq), stacked(wuk), stacked(wuv), row(LANES), row(LANES)]
        + [p[0] for p in plans],
        out_specs=[row(CONV_DIM), row(hp), row(hp), row(hp)] + [p[1] for p in plans],
        out_shape=[jax.ShapeDtypeStruct((t, CONV_DIM), BF16),
                   jax.ShapeDtypeStruct((t, hp), BF16),
                   jax.ShapeDtypeStruct((t, hp), BF16),
                   jax.ShapeDtypeStruct((t, hp), BF16)] + [p[2] for p in plans],
        scratch_shapes=[pltpu.VMEM((tm + 8, CONV_DIM), F32)],
        compiler_params=_params("arbitrary"),
        name="hyb_in",
    )(x, g, win, convw, qn, kvn, wuq, wuk, wuv, c_tab, s_tab, *[w for w, _, _ in casts])


def _attn_kernel(q_ref, k_ref, v_ref, o_ref, *, blk):
    seq = q_ref.shape[0]
    lane = lax.broadcasted_iota(jnp.int32, (blk, 2 * MLA_V), 1)
    first = lane < MLA_V
    rows = lax.broadcasted_iota(jnp.int32, (blk, blk), 0) // CHUNK
    cols = lax.broadcasted_iota(jnp.int32, (blk, blk), 1) // CHUNK
    visible = cols <= rows

    def scores(i, h):
        q0 = i * blk
        hs = slice(h * HEAD_PAD, (h + 1) * HEAD_PAD)
        s = _dot_nt(q_ref[q0:q0 + blk, hs], k_ref[0:q0 + blk, hs])
        s_diag = jnp.where(visible, s[:, q0:], -jnp.inf)
        return (jnp.concatenate([s[:, :q0], s_diag], axis=1) if i > 0 else s_diag,)

    def attend(i, h, s):
        p = jnp.exp2(s - jnp.max(s, axis=1, keepdims=True)).astype(BF16)
        pv = _dot(p, v_ref[0:(i + 1) * blk, h * HEAD_PAD:(h + 1) * HEAD_PAD])
        return pv * (1.0 / pv[:, MLA_V:MLA_V + 1])

    units = [(i, h) for i in range(seq // blk) for h in range(2)]
    pending = [scores(*unit) for unit in units[:ATTN_LOOKAHEAD]]
    outs = []
    for u, (i, h) in enumerate(units):
        if u + ATTN_LOOKAHEAD < len(units):
            pending.append(scores(*units[u + ATTN_LOOKAHEAD]))
        outs.append(attend(i, h, *pending.pop(0)))
        if h == 1:
            both = jnp.where(first, outs[0], pltpu.roll(outs[1], MLA_V, 1))
            o_ref[i * blk:(i + 1) * blk, :] = both.astype(o_ref.dtype)
            outs = []


def _attention(q, k, v, batch, seq, blk):
    t = q.shape[0]
    pairs = MLA_HEADS // 2
    spec = lambda n: pl.BlockSpec((seq, n), lambda b, p: (b, p))
    return pl.pallas_call(
        functools.partial(_attn_kernel, blk=blk),
        grid=(batch, pairs),
        in_specs=[spec(2 * HEAD_PAD), spec(2 * HEAD_PAD), spec(2 * HEAD_PAD)],
        out_specs=spec(2 * MLA_V),
        out_shape=jax.ShapeDtypeStruct((t, MLA_HEADS * MLA_V), BF16),
        compiler_params=_params("parallel", "parallel"),
        name="mla_attention",
    )(q, k, v)


def _ml_in_kernel(x_ref, g_ref, win_ref, bias_ref, q_ref, k_ref, vt_ref, o_ref, gate_ref,
                  wvt_sc, wgate_sc, *, n_chains):
    rc = x_ref.shape[0] // n_chains
    d = D_MODEL

    @pl.when(pl.program_id(0) == 0)
    def _():
        for j in range(d // ML_DH):
            cs = slice(j * ML_DH, (j + 1) * ML_DH)
            wvt_sc[cs, :] = win_ref[:, 2 * d + cs.start:2 * d + cs.stop].astype(F32).T.astype(BF16)
        wgate_sc[...] = jnp.zeros(wgate_sc.shape, BF16)
        wgate_sc[:, 0:2 * ML_HEADS] = win_ref[:, 4 * d:4 * d + 2 * ML_HEADS]

    chains = [slice(c * rc, (c + 1) * rc) for c in range(n_chains)]
    xns = [_rms(x_ref[rs, :], g_ref[...]).astype(BF16) for rs in chains]
    for rs, xn in zip(chains, xns):
        q_ref[rs, :] = (_dot(xn, win_ref[:, 0:d]) * (ML_DH ** -0.5)).astype(BF16)
        k_ref[rs, :] = _dot(xn, win_ref[:, d:2 * d]).astype(BF16)
        vt_ref[:, rs] = _dot_nt(wvt_sc[...], xn).astype(BF16)
        o_ref[rs, :] = jax.nn.sigmoid(_dot(xn, win_ref[:, 3 * d:4 * d])).astype(BF16)
        gate_ref[rs, :] = _dot(xn, wgate_sc[...]) + bias_ref[...]


def _ml_in(x, g, win, bias, batch, seq, tm, n_chains):
    t = x.shape[0]
    tps = seq // tm
    row = lambda n: pl.BlockSpec((tm, n), lambda i: (i, 0))
    act = jax.ShapeDtypeStruct((t, D_MODEL), BF16)
    return pl.pallas_call(
        functools.partial(_ml_in_kernel, n_chains=n_chains),
        grid=(t // tm,),
        in_specs=[row(D_MODEL), _resident(g), _resident(win), _resident(bias)],
        out_specs=[row(D_MODEL), row(D_MODEL),
                   pl.BlockSpec((None, D_MODEL, tm), lambda i: (i // tps, 0, i % tps)),
                   row(D_MODEL), row(LANES)],
        out_shape=[act, act, jax.ShapeDtypeStruct((batch, D_MODEL, seq), BF16), act,
                   jax.ShapeDtypeStruct((t, LANES), F32)],
        scratch_shapes=[pltpu.VMEM((D_MODEL, D_MODEL), BF16), pltpu.VMEM((D_MODEL, LANES), BF16)],
        compiler_params=_params("arbitrary"),
        name="mlstm_in",
    )(x, g, win, bias)


def _log_sigmoid(x):
    return jnp.minimum(x, 0.0) - jnp.log1p(jnp.exp(-jnp.abs(x)))


def _ml_cell_kernel(q_ref, k_ref, vt_ref, gate_ref, out_ref, st_sc, m_sc):
    nseq, ln = q_ref.shape[0], q_ref.shape[1]

    @pl.when(pl.program_id(1) == 0)
    def _():
        st_sc[...] = jnp.zeros(st_sc.shape, F32)
        m_sc[...] = jnp.zeros(m_sc.shape, F32)

    units = [(s, h) for s in range(nseq) for h in range(ML_HEADS)]
    ids = range(len(units))
    cols = lambda h: slice(h * ML_DH, (h + 1) * ML_DH)
    q = [q_ref[s, :, cols(h)] for s, h in units]
    k = [k_ref[s, :, cols(h)] for s, h in units]
    vt = [vt_ref[s, cols(h), :] for s, h in units]
    st_prev = [st_sc[u] for u in ids]

    r_idx = lax.broadcasted_iota(jnp.int32, (ln, ln), 0)
    c_idx = lax.broadcasted_iota(jnp.int32, (ln, ln), 1)
    visible = r_idx <= c_idx
    tril = jnp.where(c_idx <= r_idx, 1.0, 0.0).astype(BF16)
    first_row = lax.broadcasted_iota(jnp.int32, (ML_EXT, ln), 0) == 0
    gates, gates_t, bcum, bcum_t = [], [], [], []
    for s in range(nseq):
        g = gate_ref[s]
        lsig = _log_sigmoid(g)
        hi = lsig.astype(BF16)
        rest = lsig - hi.astype(F32)
        mid = rest.astype(BF16)
        low = (rest - mid.astype(F32)).astype(BF16)
        b = _dot(tril, hi) + _dot(tril, mid) + _dot(tril, low)
        gates.append(g)
        gates_t.append(g.T)
        bcum.append(b)
        bcum_t.append(b.T)

    kq = [_dot_nt(k[u], q[u]) for u in ids]
    carried = [_dot_nt(st_prev[u].astype(BF16), q[u]) for u in ids]

    scores, vtws, inters, recips, stats = [], [], [], [], []
    for u, (s, h) in enumerate(units):
        li_row = gates_t[s][h:h + 1, :]
        b_row = bcum_t[s][ML_HEADS + h:ML_HEADS + h + 1, :]
        bl_col = bcum[s][:, ML_HEADS + h:ML_HEADS + h + 1] - gates[s][:, h:h + 1]
        d_log = jnp.where(visible, b_row - bl_col, -jnp.inf)
        m_prev = m_sc[u][:, 0:1]
        m_t = jnp.maximum(b_row + m_prev, jnp.max(d_log, axis=0, keepdims=True))
        inter = jnp.exp(b_row + m_prev - m_t)
        sc = kq[u] * jnp.exp(d_log - m_t)
        den = inter * carried[u][ML_DH:ML_DH + 1, :] + jnp.sum(sc, axis=0, keepdims=True)
        recips.append(1.0 / jnp.maximum(jnp.abs(den), jnp.exp(-m_t)))
        scores.append(sc.astype(BF16))
        inters.append(inter)
        b_last = b_row[:, ln - 1:ln]
        a_row = b_last - b_row + li_row
        a_max = jnp.max(a_row, axis=1, keepdims=True)
        e_row = jnp.exp(a_row - a_max)
        vtws.append(jnp.concatenate([vt[u] * e_row.astype(BF16),
                                     jnp.where(first_row, e_row, 0.0).astype(BF16)], axis=0))
        m_new = jnp.maximum(b_last + m_prev, a_max)
        stats.append((jnp.exp(b_last + m_prev - m_new), jnp.exp(a_max - m_new)))
        m_sc[u] = jnp.broadcast_to(m_new, (1, LANES))

    sv = [_dot(vt[u], scores[u]) for u in ids]
    local = [_dot(vtws[u], k[u]) for u in ids]

    for u, (s, h) in enumerate(units):
        num = inters[u] * carried[u][:ML_DH, :] + sv[u]
        out_ref[s, cols(h), :] = (num * recips[u]).astype(out_ref.dtype)
        sp, sn = stats[u]
        st_sc[u] = sp * st_prev[u] + sn * local[u]


def _ml_cell(q, k, vt, gates, batch, seq, ln, nseq):
    nc = seq // ln
    units = nseq * ML_HEADS
    view = lambda a: a.reshape(batch, seq, a.shape[-1])
    blk = lambda n: pl.BlockSpec((nseq, ln, n), lambda b, c: (b, c, 0))
    blk_t = pl.BlockSpec((nseq, D_MODEL, ln), lambda b, c: (b, 0, c))
    return pl.pallas_call(
        _ml_cell_kernel,
        grid=(batch // nseq, nc),
        in_specs=[blk(D_MODEL), blk(D_MODEL), blk_t, blk(LANES)],
        out_specs=blk_t,
        out_shape=jax.ShapeDtypeStruct((batch, D_MODEL, seq), BF16),
        scratch_shapes=[pltpu.VMEM((units, ML_DH + ML_EXT, ML_DH), F32),
                        pltpu.VMEM((units, 1, LANES), F32)],
        compiler_params=_params("parallel", "arbitrary"),
        name="mlstm_cell",
    )(view(q), view(k), vt, view(gates))


def _out_mlp_kernel(*refs, n_mix, n_cast, gated, chain_rows):
    mix = refs[:n_mix]
    wout_ref, x_ref, gpost_ref, gpre_ref, gmlp_ref, w1_ref, w2_ref = refs[n_mix:n_mix + 7]
    out_ref = refs[n_mix + 7 + n_cast]
    _cast_slices(refs[n_mix + 7:n_mix + 7 + n_cast] + refs[n_mix + 8 + n_cast:])
    starts = np.cumsum((0,) + tuple(chain_rows))
    chains = [slice(int(a), int(b)) for a, b in zip(starts[:-1], starts[1:])]

    def mixed(rs):
        if not gated:
            return jnp.concatenate([p[rs, :] for p in mix], axis=1)
        cell_ref, gate_ref, hnorm_ref = mix
        heads = []
        for h in range(ML_HEADS):
            cs = slice(h * ML_DH, (h + 1) * ML_DH)
            ct = cell_ref[cs, rs].astype(F32)
            scale = lax.rsqrt(jnp.mean(ct * ct, axis=0, keepdims=True) + EPS)
            normed = (ct * scale).T * hnorm_ref[:, cs]
            heads.append((gate_ref[rs, cs].astype(F32) * normed).astype(BF16))
        return jnp.concatenate(heads, axis=1)

    x1s = []
    for rs in chains:
        y = _dot(mixed(rs), wout_ref[...])
        x1s.append(x_ref[rs, :] + _rms(y, gpost_ref[...]))
    for rs, x1 in zip(chains, x1s):
        hn = _rms(x1, gpre_ref[...]).astype(BF16)
        a = jnp.square(jnp.maximum(_dot(hn, w1_ref[...]), 0.0)).astype(BF16)
        out_ref[rs, :] = x1 + _rms(_dot(a, w2_ref[...]), gmlp_ref[...])


def _out_mlp(mix, gated, wout, x, gpost, gpre, gmlp, w1, w2, casts, tm, chain_rows):
    t = x.shape[0]
    row = lambda n: pl.BlockSpec((tm, n), lambda i: (i, 0))
    vec = _resident(gpost)
    plans = [_cast_plan(w, l, axis, t // tm) for w, l, axis in casts]
    if gated:
        tps = mix[0].shape[2] // tm
        mix_specs = [pl.BlockSpec((None, D_MODEL, tm), lambda i: (i // tps, 0, i % tps)),
                     row(D_MODEL), vec]
    else:
        mix_specs = [row(p.shape[1]) for p in mix]
    return pl.pallas_call(
        functools.partial(_out_mlp_kernel, n_mix=len(mix), n_cast=len(casts), gated=gated,
                          chain_rows=chain_rows),
        grid=(t // tm,),
        in_specs=mix_specs
        + [_resident(wout), row(D_MODEL), vec, vec, vec, _resident(w1), _resident(w2)]
        + [p[0] for p in plans],
        out_specs=[row(D_MODEL)] + [p[1] for p in plans],
        out_shape=[jax.ShapeDtypeStruct((t, D_MODEL), F32)] + [p[2] for p in plans],
        compiler_params=_params("parallel"),
        name="out_mlp",
    )(*mix, wout, x, gpost, gpre, gmlp, w1, w2, *[w for w, _, _ in casts])


def _hyb_weights(w_in, w_uq, w_ukv):
    n = w_in.shape[0]
    split = 3 * CONV_DIM + 2 * MLA_RANK
    src = _head_lanes()
    heads = lambda w: w.astype(BF16).reshape(n, MLA_RANK, MLA_HEADS * HEAD_PAD)
    k_r = _place(w_in[:, :, split:], np.where(src >= MLA_NOPE, src - MLA_NOPE, -1))
    win = jnp.concatenate([w_in[:, :, :split], k_r], axis=2).astype(BF16)
    wuq = heads(_place(w_uq.reshape(n, MLA_RANK, MLA_HEADS, MLA_NOPE + MLA_ROPE), src))
    ukv = w_ukv.reshape(n, MLA_RANK, MLA_HEADS, MLA_NOPE + MLA_V)
    wuk = heads(_place(ukv[..., :MLA_NOPE], src))
    wuv = heads(_place(ukv[..., MLA_NOPE:], np.where(np.arange(HEAD_PAD) < MLA_V, np.arange(HEAD_PAD), -1)))
    return win, wuq, wuk, wuv


def kernel(x, positions, norm_mix_pre, norm_mix_post, norm_mlp_pre, norm_mlp_post, hyb_w_in, conv_w, mla_q_norm, mla_kv_norm, mla_w_uq, mla_w_ukv, hyb_w_out, ml_w_in, ml_b_i, ml_b_f, ml_head_norm, ml_w_out, mlp_w1, mlp_w2):
    batch, seq, d = x.shape
    t = batch * seq
    depth = norm_mix_pre.shape[0]
    tm_in = min(1024, seq)
    in_chains = 4
    tm_mlp = min(1024, seq)
    mlp_chains = (tm_mlp // 4,) * 4
    blk = min(256, seq)
    cell_seqs = 4 if batch % 4 == 0 else 1

    xt = x.reshape(t, d)
    c_tab, s_tab = _rope_tables(positions)
    vec = lambda a: a.reshape(1, -1)

    hyb_win, hyb_wuq, hyb_wuk, hyb_wuv = _hyb_weights(hyb_w_in, mla_w_uq, mla_w_ukv)

    def layer_casts(l):
        if l >= depth:
            return []
        e = l // 2
        casts = [(hyb_w_out if l % 2 == 0 else ml_w_out, e, 0), (mlp_w1, l, 1), (mlp_w2, l, 0)]
        return casts + ([(ml_w_in, e, 0)] if l % 2 == 1 else [])

    ready = None
    for l in range(depth):
        e = l // 2
        if l % 2 == 0:
            y_a, q, k, v, *own = _hyb_in(
                xt, vec(norm_mix_pre[l]), hyb_win, conv_w[e], vec(mla_q_norm[e]),
                vec(mla_kv_norm[e]), hyb_wuq, hyb_wuk, hyb_wuv, e, c_tab, s_tab,
                layer_casts(l) if ready is None else [], seq, tm_in, in_chains)
            wout, w1, w2 = own if ready is None else ready
            y_b = _attention(q, k, v, batch, seq, blk)
            mix = [y_a, y_b]
        else:
            wout, w1, w2, ml_win = ready
            bias = jnp.pad(jnp.concatenate([ml_b_i[e], ml_b_f[e]]), (0, LANES - 2 * ML_HEADS))[None, :]
            q, k, vt, o_gate, gates = _ml_in(
                xt, vec(norm_mix_pre[l]), ml_win, bias, batch, seq, tm_in, in_chains)
            cell = _ml_cell(q, k, vt, gates, batch, seq, blk, cell_seqs)
            mix = [cell, o_gate, vec(ml_head_norm[e])]
        xt, *ready = _out_mlp(mix, l % 2 == 1, wout, xt, vec(norm_mix_post[l]),
                              vec(norm_mlp_pre[l]), vec(norm_mlp_post[l]), w1, w2,
                              layer_casts(l + 1), tm_mlp, mlp_chains)
    return xt.reshape(batch, seq, d)
```

```python
import functools

import jax
import jax.numpy as jnp
import numpy as np
from jax import lax
from jax.experimental import pallas as pl
from jax.experimental.pallas import tpu as pltpu

D_MODEL = 1024
EPS = 1e-6
CHUNK = 64
CONV_DIM = 512
MLA_HEADS = 8
MLA_RANK = 256
MLA_NOPE = 64
MLA_ROPE = 32
MLA_V = 64
ROPE_THETA = 10000.0
ML_HEADS = 4
ML_DH = 256

ATTN_SCALE = (MLA_NOPE + MLA_ROPE) ** -0.5
LOG2_E = 1.4426950408889634

LANES = 128
HEAD_PAD = LANES
ML_EXT = 16
ATTN_LOOKAHEAD = 3
VMEM_LIMIT = 56 * 1024 * 1024

BF16 = jnp.bfloat16
F32 = jnp.float32


def _dot(a, b):
    return jnp.dot(a, b, preferred_element_type=F32)


def _dot_nt(a, b):
    return lax.dot_general(a, b, (((1,), (1,)), ((), ())), preferred_element_type=F32)


def _rms(x, g):
    return x * lax.rsqrt(jnp.mean(x * x, axis=-1, keepdims=True) + EPS) * g


def _params(*semantics):
    return pltpu.CompilerParams(dimension_semantics=semantics, vmem_limit_bytes=VMEM_LIMIT)


def _resident(a, layer=None):
    if layer is None:
        return pl.BlockSpec(a.shape, lambda *_: (0,) * a.ndim, pipeline_mode=pl.Buffered(1))
    return pl.BlockSpec((None,) + a.shape[1:], lambda *_: (layer,) + (0,) * (a.ndim - 1),
                        pipeline_mode=pl.Buffered(1))


def _cast_plan(w, layer, axis, steps):
    r, c = w.shape[1:]
    if axis == 0:
        blk, imap_in, imap_out = (r // steps, c), (lambda i: (layer, i, 0)), (lambda i: (i, 0))
    else:
        blk, imap_in, imap_out = (r, c // steps), (lambda i: (layer, 0, i)), (lambda i: (0, i))
    return (pl.BlockSpec((None,) + blk, imap_in), pl.BlockSpec(blk, imap_out),
            jax.ShapeDtypeStruct((r, c), BF16))


def _cast_slices(refs):
    n = len(refs) // 2
    for src, dst in zip(refs[:n], refs[n:]):
        dst[...] = src[...].astype(BF16)


def _rope_table_kernel(pos_ref, lane_ref, c_ref, s_ref):
    rows = pos_ref.shape[0]
    half = MLA_ROPE // 2
    inv, m1, m2, m0 = (lane_ref[r:r + 1, :] for r in range(4))
    ang = pos_ref[...].astype(F32) * inv
    cos, sin = jnp.cos(ang), jnp.sin(ang)
    for g in range(LANES // half):
        to1 = (LANES // 2 - half - g * half) % LANES
        to2 = (LANES - half - g * half) % LANES
        out = slice(g * rows, (g + 1) * rows)
        c_ref[out, :] = m0 + pltpu.roll(cos, to1, 1) * m1 + pltpu.roll(cos, to2, 1) * m2
        s_ref[out, :] = pltpu.roll(sin, to2, 1) * m2 - pltpu.roll(sin, to1, 1) * m1


def _head_lanes():
    half = MLA_ROPE // 2
    x1 = LANES // 2 - half
    src = -np.ones((LANES,), np.int32)
    src[:x1] = np.arange(x1)
    src[x1:x1 + half] = MLA_NOPE + np.arange(half)
    src[LANES // 2:LANES // 2 + MLA_NOPE - x1] = x1 + np.arange(MLA_NOPE - x1)
    src[LANES - half:] = MLA_NOPE + half + np.arange(half)
    return src


def _place(w, src):
    n = w.shape[-1]
    idx = np.where((src >= 0) & (src < n), src, n)
    return jnp.take(jnp.pad(w, [(0, 0)] * (w.ndim - 1) + [(0, 1)]), idx, axis=-1)


def _rope_tables(positions):
    t = positions.size
    half = MLA_ROPE // 2
    groups = LANES // half
    inv = ROPE_THETA ** (-jnp.arange(half, dtype=F32) / half)
    rope_src = _head_lanes() - MLA_NOPE
    first = ((rope_src >= 0) & (rope_src < half)).astype(np.float32)
    second = (rope_src >= half).astype(np.float32)
    lanes = jnp.stack([jnp.tile(inv, groups), jnp.asarray(first), jnp.asarray(second),
                       jnp.asarray(1.0 - first - second)]
                      + [jnp.zeros((LANES,), F32)] * 4)
    pos = jnp.repeat(positions.reshape(groups, t // groups).T, half, axis=1)
    return pl.pallas_call(
        _rope_table_kernel,
        out_shape=[jax.ShapeDtypeStruct((t, LANES), F32)] * 2,
        compiler_params=pltpu.CompilerParams(vmem_limit_bytes=VMEM_LIMIT),
        name="rope_tables",
    )(pos, lanes)


def _rope(v, c, s):
    out = []
    for h in range(v.shape[1] // HEAD_PAD):
        vb = v[:, h * HEAD_PAD:(h + 1) * HEAD_PAD]
        out.append(vb * c + pltpu.roll(vb, HEAD_PAD // 2, 1) * s)
    return out[0] if len(out) == 1 else jnp.concatenate(out, axis=1)


def _hyb_in_kernel(x_ref, g_ref, win_ref, convw_ref, qn_ref, kvn_ref, wuq_ref, wuk_ref,
                   wuv_ref, c_ref, s_ref, *rest, tiles_per_seq, n_chains, q_scale):
    n_cast = (len(rest) - 5) // 2
    ya_ref, q_ref, k_ref, v_ref = rest[n_cast:n_cast + 4]
    u_sc = rest[-1]
    _cast_slices(rest[:n_cast] + rest[n_cast + 4:-1])
    tm = x_ref.shape[0]
    rc = tm // n_chains
    cd, r = CONV_DIM, MLA_RANK

    @pl.when(pl.program_id(0) % tiles_per_seq == 0)
    def _():
        u_sc[0:8, :] = jnp.zeros((8, cd), F32)

    @pl.when(pl.program_id(0) % tiles_per_seq != 0)
    def _():
        u_sc[0:8, :] = u_sc[tm:tm + 8, :]

    chains = [slice(c * rc, (c + 1) * rc) for c in range(n_chains)]
    xns = [_rms(x_ref[rs, :], g_ref[...]).astype(BF16) for rs in chains]
    w = convw_ref[...]
    def down(rs, xn):
        b_gate = _dot(xn, win_ref[:, 0:cd])
        u = _dot(xn, win_ref[:, cd:2 * cd]) * _dot(xn, win_ref[:, 2 * cd:3 * cd])
        u_sc[8 + rs.start:8 + rs.stop, :] = u
        y = (w[2:3, :] * u + w[1:2, :] * u_sc[7 + rs.start:7 + rs.stop, :]
             + w[0:1, :] * u_sc[6 + rs.start:6 + rs.stop, :])
        ya_ref[rs, :] = (b_gate * y).astype(BF16)
        cq = _rms(_dot(xn, win_ref[:, 3 * cd:3 * cd + r]), qn_ref[...]).astype(BF16)
        ckv = _rms(_dot(xn, win_ref[:, 3 * cd + r:3 * cd + 2 * r]), kvn_ref[...]).astype(BF16)
        kr = _dot(xn, win_ref[:, 3 * cd + 2 * r:])
        return cq, ckv, kr

    head_lane = lax.broadcasted_iota(jnp.int32, (1, MLA_HEADS * HEAD_PAD), 1) % HEAD_PAD
    v_one = jnp.where(head_lane == MLA_V, 1.0, 0.0)

    def up(rs, cq, ckv, kr):
        c, s = c_ref[rs, :], s_ref[rs, :]
        q_ref[rs, :] = (_rope(_dot(cq, wuq_ref[...]), c, s) * q_scale).astype(BF16)
        k_rope = _rope(kr, c, s)
        k_ref[rs, :] = (_dot(ckv, wuk_ref[...])
                        + jnp.concatenate([k_rope] * MLA_HEADS, axis=1)).astype(BF16)
        v_ref[rs, :] = (_dot(ckv, wuv_ref[...]) + v_one).astype(BF16)

    low = None
    for c, (rs, xn) in enumerate(zip(chains, xns)):
        nxt = down(rs, xn)
        if low is not None:
            up(chains[c - 1], *low)
        low = nxt
    up(chains[-1], *low)


def _hyb_in(x, g, win, convw, qn, kvn, wuq, wuk, wuv, layer, c_tab, s_tab, casts, seq, tm,
            n_chains):
    t = x.shape[0]
    hp = MLA_HEADS * HEAD_PAD
    row = lambda n: pl.BlockSpec((tm, n), lambda i: (i, 0))
    full = _resident
    stacked = lambda a: _resident(a, layer)
    plans = [_cast_plan(w, l, axis, t // tm) for w, l, axis in casts]
    return pl.pallas_call(
        functools.partial(_hyb_in_kernel, tiles_per_seq=seq // tm, n_chains=n_chains,
                          q_scale=ATTN_SCALE * LOG2_E),
        grid=(t // tm,),
        in_specs=[row(D_MODEL), full(g), stacked(win), full(convw), full(qn), full(kvn),
                  stacked(wuq), stacked(wuk), stacked(wuv), row(LANES), row(LANES)]
        + [p[0] for p in plans],
        out_specs=[row(CONV_DIM), row(hp), row(hp), row(hp)] + [p[1] for p in plans],
        out_shape=[jax.ShapeDtypeStruct((t, CONV_DIM), BF16),
                   jax.ShapeDtypeStruct((t, hp), BF16),
                   jax.ShapeDtypeStruct((t, hp), BF16),
                   jax.ShapeDtypeStruct((t, hp), BF16)] + [p[2] for p in plans],
        scratch_shapes=[pltpu.VMEM((tm + 8, CONV_DIM), F32)],
        compiler_params=_params("arbitrary"),
        name="hyb_in",
    )(x, g, win, convw, qn, kvn, wuq, wuk, wuv, c_tab, s_tab, *[w for w, _, _ in casts])


def _attn_kernel(q_ref, k_ref, v_ref, o_ref, *, blk):
    seq = q_ref.shape[0]
    lane = lax.broadcasted_iota(jnp.int32, (blk, 2 * MLA_V), 1)
    first = lane < MLA_V
    rows = lax.broadcasted_iota(jnp.int32, (blk, blk), 0) // CHUNK
    cols = lax.broadcasted_iota(jnp.int32, (blk, blk), 1) // CHUNK
    visible = cols <= rows

    def scores(i, h):
        q0 = i * blk
        hs = slice(h * HEAD_PAD, (h + 1) * HEAD_PAD)
        s = _dot_nt(q_ref[q0:q0 + blk, hs], k_ref[0:q0 + blk, hs])
        s_diag = jnp.where(visible, s[:, q0:], -jnp.inf)
        return (jnp.concatenate([s[:, :q0], s_diag], axis=1) if i > 0 else s_diag,)

    def attend(i, h, s):
        p = jnp.exp2(s - jnp.max(s, axis=1, keepdims=True)).astype(BF16)
        pv = _dot(p, v_ref[0:(i + 1) * blk, h * HEAD_PAD:(h + 1) * HEAD_PAD])
        return pv * (1.0 / pv[:, MLA_V:MLA_V + 1])

    units = [(i, h) for i in range(seq // blk) for h in range(2)]
    pending = [scores(*unit) for unit in units[:ATTN_LOOKAHEAD]]
    outs = []
    for u, (i, h) in enumerate(units):
        if u + ATTN_LOOKAHEAD < len(units):
            pending.append(scores(*units[u + ATTN_LOOKAHEAD]))
        outs.append(attend(i, h, *pending.pop(0)))
        if h == 1:
            both = jnp.where(first, outs[0], pltpu.roll(outs[1], MLA_V, 1))
            o_ref[i * blk:(i + 1) * blk, :] = both.astype(o_ref.dtype)
            outs = []


def _attention(q, k, v, batch, seq, blk):
    t = q.shape[0]
    pairs = MLA_HEADS // 2
    spec = lambda n: pl.BlockSpec((seq, n), lambda b, p: (b, p))
    return pl.pallas_call(
        functools.partial(_attn_kernel, blk=blk),
        grid=(batch, pairs),
        in_specs=[spec(2 * HEAD_PAD), spec(2 * HEAD_PAD), spec(2 * HEAD_PAD)],
        out_specs=spec(2 * MLA_V),
        out_shape=jax.ShapeDtypeStruct((t, MLA_HEADS * MLA_V), BF16),
        compiler_params=_params("parallel", "parallel"),
        name="mla_attention",
    )(q, k, v)


def _ml_in_kernel(x_ref, g_ref, win_ref, bias_ref, q_ref, k_ref, vt_ref, o_ref, gate_ref,
                  wvt_sc, wgate_sc, *, n_chains):
    rc = x_ref.shape[0] // n_chains
    d = D_MODEL

    @pl.when(pl.program_id(0) == 0)
    def _():
        for j in range(d // ML_DH):
            cs = slice(j * ML_DH, (j + 1) * ML_DH)
            wvt_sc[cs, :] = win_ref[:, 2 * d + cs.start:2 * d + cs.stop].astype(F32).T.astype(BF16)
        wgate_sc[...] = jnp.zeros(wgate_sc.shape, BF16)
        wgate_sc[:, 0:2 * ML_HEADS] = win_ref[:, 4 * d:4 * d + 2 * ML_HEADS]

    chains = [slice(c * rc, (c + 1) * rc) for c in range(n_chains)]
    xns = [_rms(x_ref[rs, :], g_ref[...]).astype(BF16) for rs in chains]
    for rs, xn in zip(chains, xns):
        q_ref[rs, :] = (_dot(xn, win_ref[:, 0:d]) * (ML_DH ** -0.5)).astype(BF16)
        k_ref[rs, :] = _dot(xn, win_ref[:, d:2 * d]).astype(BF16)
        vt_ref[:, rs] = _dot_nt(wvt_sc[...], xn).astype(BF16)
        o_ref[rs, :] = jax.nn.sigmoid(_dot(xn, win_ref[:, 3 * d:4 * d])).astype(BF16)
        gate_ref[rs, :] = _dot(xn, wgate_sc[...]) + bias_ref[...]


def _ml_in(x, g, win, bias, batch, seq, tm, n_chains):
    t = x.shape[0]
    tps = seq // tm
    row = lambda n: pl.BlockSpec((tm, n), lambda i: (i, 0))
    act = jax.ShapeDtypeStruct((t, D_MODEL), BF16)
    return pl.pallas_call(
        functools.partial(_ml_in_kernel, n_chains=n_chains),
        grid=(t // tm,),
        in_specs=[row(D_MODEL), _resident(g), _resident(win), _resident(bias)],
        out_specs=[row(D_MODEL), row(D_MODEL),
                   pl.BlockSpec((None, D_MODEL, tm), lambda i: (i // tps, 0, i % tps)),
                   row(D_MODEL), row(LANES)],
        out_shape=[act, act, jax.ShapeDtypeStruct((batch, D_MODEL, seq), BF16), act,
                   jax.ShapeDtypeStruct((t, LANES), F32)],
        scratch_shapes=[pltpu.VMEM((D_MODEL, D_MODEL), BF16), pltpu.VMEM((D_MODEL, LANES), BF16)],
        compiler_params=_params("arbitrary"),
        name="mlstm_in",
    )(x, g, win, bias)


def _log_sigmoid(x):
    return jnp.minimum(x, 0.0) - jnp.log1p(jnp.exp(-jnp.abs(x)))


def _ml_cell_kernel(q_ref, k_ref, vt_ref, gate_ref, out_ref, st_sc, m_sc):
    nseq, ln = q_ref.shape[0], q_ref.shape[1]

    @pl.when(pl.program_id(1) == 0)
    def _():
        st_sc[...] = jnp.zeros(st_sc.shape, F32)
        m_sc[...] = jnp.zeros(m_sc.shape, F32)

    units = [(s, h) for s in range(nseq) for h in range(ML_HEADS)]
    ids = range(len(units))
    cols = lambda h: slice(h * ML_DH, (h + 1) * ML_DH)
    q = [q_ref[s, :, cols(h)] for s, h in units]
    k = [k_ref[s, :, cols(h)] for s, h in units]
    vt = [vt_ref[s, cols(h), :] for s, h in units]
    st_prev = [st_sc[u] for u in ids]

    r_idx = lax.broadcasted_iota(jnp.int32, (ln, ln), 0)
    c_idx = lax.broadcasted_iota(jnp.int32, (ln, ln), 1)
    visible = r_idx <= c_idx
    tril = jnp.where(c_idx <= r_idx, 1.0, 0.0).astype(BF16)
    first_row = lax.broadcasted_iota(jnp.int32, (ML_EXT, ln), 0) == 0
    gates, gates_t, bcum, bcum_t = [], [], [], []
    for s in range(nseq):
        g = gate_ref[s]
        lsig = _log_sigmoid(g)
        hi = lsig.astype(BF16)
        rest = lsig - hi.astype(F32)
        mid = rest.astype(BF16)
        low = (rest - mid.astype(F32)).astype(BF16)
        b = _dot(tril, hi) + _dot(tril, mid) + _dot(tril, low)
        gates.append(g)
        gates_t.append(g.T)
        bcum.append(b)
        bcum_t.append(b.T)

    kq = [_dot_nt(k[u], q[u]) for u in ids]
    carried = [_dot_nt(st_prev[u].astype(BF16), q[u]) for u in ids]

    scores, vtws, inters, recips, stats = [], [], [], [], []
    for u, (s, h) in enumerate(units):
        li_row = gates_t[s][h:h + 1, :]
        b_row = bcum_t[s][ML_HEADS + h:ML_HEADS + h + 1, :]
        bl_col = bcum[s][:, ML_HEADS + h:ML_HEADS + h + 1] - gates[s][:, h:h + 1]
        d_log = jnp.where(visible, b_row - bl_col, -jnp.inf)
        m_prev = m_sc[u][:, 0:1]
        m_t = jnp.maximum(b_row + m_prev, jnp.max(d_log, axis=0, keepdims=True))
        inter = jnp.exp(b_row + m_prev - m_t)
        sc = kq[u] * jnp.exp(d_log - m_t)
        den = inter * carried[u][ML_DH:ML_DH + 1, :] + jnp.sum(sc, axis=0, keepdims=True)
        recips.append(1.0 / jnp.maximum(jnp.abs(den), jnp.exp(-m_t)))
        scores.append(sc.astype(BF16))
        inters.append(inter)
        b_last = b_row[:, ln - 1:ln]
        a_row = b_last - b_row + li_row
        a_max = jnp.max(a_row, axis=1, keepdims=True)
        e_row = jnp.exp(a_row - a_max)
        vtws.append(jnp.concatenate([vt[u] * e_row.astype(BF16),
                                     jnp.where(first_row, e_row, 0.0).astype(BF16)], axis=0))
        m_new = jnp.maximum(b_last + m_prev, a_max)
        stats.append((jnp.exp(b_last + m_prev - m_new), jnp.exp(a_max - m_new)))
        m_sc[u] = jnp.broadcast_to(m_new, (1, LANES))

    sv = [_dot(vt[u], scores[u]) for u in ids]
    local = [_dot(vtws[u], k[u]) for u in ids]

    for u, (s, h) in enumerate(units):
        num = inters[u] * carried[u][:ML_DH, :] + sv[u]
        out_ref[s, cols(h), :] = (num * recips[u]).astype(out_ref.dtype)
        sp, sn = stats[u]
        st_sc[u] = sp * st_prev[u] + sn * local[u]


def _ml_cell(q, k, vt, gates, batch, seq, ln, nseq):
    nc = seq // ln
    units = nseq * ML_HEADS
    view = lambda a: a.reshape(batch, seq, a.shape[-1])
    blk = lambda n: pl.BlockSpec((nseq, ln, n), lambda b, c: (b, c, 0))
    blk_t = pl.BlockSpec((nseq, D_MODEL, ln), lambda b, c: (b, 0, c))
    return pl.pallas_call(
        _ml_cell_kernel,
        grid=(batch // nseq, nc),
        in_specs=[blk(D_MODEL), blk(D_MODEL), blk_t, blk(LANES)],
        out_specs=blk_t,
        out_shape=jax.ShapeDtypeStruct((batch, D_MODEL, seq), BF16),
        scratch_shapes=[pltpu.VMEM((units, ML_DH + ML_EXT, ML_DH), F32),
                        pltpu.VMEM((units, 1, LANES), F32)],
        compiler_params=_params("parallel", "arbitrary"),
        name="mlstm_cell",
    )(view(q), view(k), vt, view(gates))


def _out_mlp_kernel(*refs, n_mix, n_cast, gated, chain_rows):
    mix = refs[:n_mix]
    wout_ref, x_ref, gpost_ref, gpre_ref, gmlp_ref, w1_ref, w2_ref = refs[n_mix:n_mix + 7]
    out_ref = refs[n_mix + 7 + n_cast]
    _cast_slices(refs[n_mix + 7:n_mix + 7 + n_cast] + refs[n_mix + 8 + n_cast:])
    starts = np.cumsum((0,) + tuple(chain_rows))
    chains = [slice(int(a), int(b)) for a, b in zip(starts[:-1], starts[1:])]

    def mixed(rs):
        if not gated:
            return jnp.concatenate([p[rs, :] for p in mix], axis=1)
        cell_ref, gate_ref, hnorm_ref = mix
        heads = []
        for h in range(ML_HEADS):
            cs = slice(h * ML_DH, (h + 1) * ML_DH)
            ct = cell_ref[cs, rs].astype(F32)
            scale = lax.rsqrt(jnp.mean(ct * ct, axis=0, keepdims=True) + EPS)
            normed = (ct * scale).T * hnorm_ref[:, cs]
            heads.append((gate_ref[rs, cs].astype(F32) * normed).astype(BF16))
        return jnp.concatenate(heads, axis=1)

    x1s = []
    for rs in chains:
        y = _dot(mixed(rs), wout_ref[...])
        x1s.append(x_ref[rs, :] + _rms(y, gpost_ref[...]))
    for rs, x1 in zip(chains, x1s):
        hn = _rms(x1, gpre_ref[...]).astype(BF16)
        a = jnp.square(jnp.maximum(_dot(hn, w1_ref[...]), 0.0)).astype(BF16)
        out_ref[rs, :] = x1 + _rms(_dot(a, w2_ref[...]), gmlp_ref[...])


def _out_mlp(mix, gated, wout, x, gpost, gpre, gmlp, w1, w2, casts, tm, chain_rows):
    t = x.shape[0]
    row = lambda n: pl.BlockSpec((tm, n), lambda i: (i, 0))
    vec = _resident(gpost)
    plans = [_cast_plan(w, l, axis, t // tm) for w, l, axis in casts]
    if gated:
        tps = mix[0].shape[2] // tm
        mix_specs = [pl.BlockSpec((None, D_MODEL, tm), lambda i: (i // tps, 0, i % tps)),
                     row(D_MODEL), vec]
    else:
        mix_specs = [row(p.shape[1]) for p in mix]
    return pl.pallas_call(
        functools.partial(_out_mlp_kernel, n_mix=len(mix), n_cast=len(casts), gated=gated,
                          chain_rows=chain_rows),
        grid=(t // tm,),
        in_specs=mix_specs
        + [_resident(wout), row(D_MODEL), vec, vec, vec, _resident(w1), _resident(w2)]
        + [p[0] for p in plans],
        out_specs=[row(D_MODEL)] + [p[1] for p in plans],
        out_shape=[jax.ShapeDtypeStruct((t, D_MODEL), F32)] + [p[2] for p in plans],
        compiler_params=_params("parallel"),
        name="out_mlp",
    )(*mix, wout, x, gpost, gpre, gmlp, w1, w2, *[w for w, _, _ in casts])


def _hyb_weights(w_in, w_uq, w_ukv):
    n = w_in.shape[0]
    split = 3 * CONV_DIM + 2 * MLA_RANK
    src = _head_lanes()
    heads = lambda w: w.astype(BF16).reshape(n, MLA_RANK, MLA_HEADS * HEAD_PAD)
    k_r = _place(w_in[:, :, split:], np.where(src >= MLA_NOPE, src - MLA_NOPE, -1))
    win = jnp.concatenate([w_in[:, :, :split], k_r], axis=2).astype(BF16)
    wuq = heads(_place(w_uq.reshape(n, MLA_RANK, MLA_HEADS, MLA_NOPE + MLA_ROPE), src))
    ukv = w_ukv.reshape(n, MLA_RANK, MLA_HEADS, MLA_NOPE + MLA_V)
    wuk = heads(_place(ukv[..., :MLA_NOPE], src))
    wuv = heads(_place(ukv[..., MLA_NOPE:], np.where(np.arange(HEAD_PAD) < MLA_V, np.arange(HEAD_PAD), -1)))
    return win, wuq, wuk, wuv


def kernel(x, positions, norm_mix_pre, norm_mix_post, norm_mlp_pre, norm_mlp_post, hyb_w_in, conv_w, mla_q_norm, mla_kv_norm, mla_w_uq, mla_w_ukv, hyb_w_out, ml_w_in, ml_b_i, ml_b_f, ml_head_norm, ml_w_out, mlp_w1, mlp_w2):
    batch, seq, d = x.shape
    t = batch * seq
    depth = norm_mix_pre.shape[0]
    tm_in = min(1024, seq)
    in_chains = 4
    tm_mlp = min(1024, seq)
    mlp_chains = (tm_mlp // 4,) * 4
    blk = min(256, seq)
    cell_seqs = 4 if batch % 4 == 0 else 1

    xt = x.reshape(t, d)
    c_tab, s_tab = _rope_tables(positions)
    vec = lambda a: a.reshape(1, -1)

    hyb_win, hyb_wuq, hyb_wuk, hyb_wuv = _hyb_weights(hyb_w_in, mla_w_uq, mla_w_ukv)

    def layer_casts(l):
        if l >= depth:
            return []
        e = l // 2
        casts = [(hyb_w_out if l % 2 == 0 else ml_w_out, e, 0), (mlp_w1, l, 1), (mlp_w2, l, 0)]
        return casts + ([(ml_w_in, e, 0)] if l % 2 == 1 else [])

    ready = None
    for l in range(depth):
        e = l // 2
        if l % 2 == 0:
            y_a, q, k, v, *own = _hyb_in(
                xt, vec(norm_mix_pre[l]), hyb_win, conv_w[e], vec(mla_q_norm[e]),
                vec(mla_kv_norm[e]), hyb_wuq, hyb_wuk, hyb_wuv, e, c_tab, s_tab,
                layer_casts(l) if ready is None else [], seq, tm_in, in_chains)
            wout, w1, w2 = own if ready is None else ready
            y_b = _attention(q, k, v, batch, seq, blk)
            mix = [y_a, y_b]
        else:
            wout, w1, w2, ml_win = ready
            bias = jnp.pad(jnp.concatenate([ml_b_i[e], ml_b_f[e]]), (0, LANES - 2 * ML_HEADS))[None, :]
            q, k, vt, o_gate, gates = _ml_in(
                xt, vec(norm_mix_pre[l]), ml_win, bias, batch, seq, tm_in, in_chains)
            cell = _ml_cell(q, k, vt, gates, batch, seq, blk, cell_seqs)
            mix = [cell, o_gate, vec(ml_head_norm[e])]
        xt, *ready = _out_mlp(mix, l % 2 == 1, wout, xt, vec(norm_mix_post[l]),
                              vec(norm_mlp_pre[l]), vec(norm_mlp_post[l]), w1, w2,
                              layer_casts(l + 1), tm_mlp, mlp_chains)
    return xt.reshape(batch, seq, d)
```
